```python
import jax
import jax.numpy as jnp
from jax import lax
import numpy as np

D_MODEL = 2048
BATCH = 2
SEQ = 8192
DEPTH = 2

MIX_WIDTH = D_MODEL
D_A = MIX_WIDTH // 2
D_B = MIX_WIDTH // 2
D_C = MIX_WIDTH // 2
D_D = MIX_WIDTH // 2
POOL_WINDOWS = (2, 4, 8, 16)
POOL_GROUPS = len(POOL_WINDOWS)
POOL_GC = D_A // POOL_GROUPS
SHORT_CONV = 3
SGU_CHUNK = 128
SGU_HEADS = 8
SGU_HD = D_C // SGU_HEADS
CONF_KERNEL = 31
CONF_GROUPS = 8
CONF_GD = D_D // CONF_GROUPS
PEER_HEADS = 8
PEER_NKEYS = 128
PEER_EXPERTS = PEER_NKEYS * PEER_NKEYS
PEER_TOPK = 16
PEER_DK = 128
PEER_DKH = PEER_DK // 2
PEER_TOKEN_CHUNK = 128
EPS = 1e-6

kernel_name = 'hybrid_pool_shortconv_sgu_conformer_peer'


def rms_norm(x, g):
    x32 = x.astype(jnp.float32)
    y = x32 * lax.rsqrt(jnp.mean(x32 * x32, axis=-1, keepdims=True) + EPS)
    return (y * g).astype(x.dtype)


def layer_norm(x, g, b):
    x32 = x.astype(jnp.float32)
    mu = jnp.mean(x32, axis=-1, keepdims=True)
    xc = x32 - mu
    y = xc * lax.rsqrt(jnp.mean(xc * xc, axis=-1, keepdims=True) + EPS)
    return (y * g + b).astype(x.dtype)


def causal_dwconv(x, w):
    k, c = w.shape
    return lax.conv_general_dilated(
        x, w[:, None, :], window_strides=(1,), padding=((k - 1, 0),),
        dimension_numbers=('NWC', 'WIO', 'NWC'), feature_group_count=c)


def pool_mixer(xa, w, scale):
    bsz, s, _ = xa.shape
    xg = xa.astype(jnp.float32).reshape(bsz, s, POOL_GROUPS, POOL_GC)
    csum = jnp.cumsum(xg, axis=1)
    pos = jnp.arange(s, dtype=jnp.int32)
    outs = []
    for g, win in enumerate(POOL_WINDOWS):
        c = csum[:, :, g]
        lag = jnp.pad(c, ((0, 0), (win, 0), (0, 0)))[:, :s]
        cnt = jnp.minimum(pos + 1, win).astype(jnp.float32)[None, :, None]
        outs.append((c - lag) / cnt - xg[:, :, g])
    pooled = jnp.stack(outs, axis=2).astype(xa.dtype)
    mixed = jnp.einsum('bsgc,gcd->bsgd', pooled, w).reshape(bsz, s, D_A)
    return mixed * scale


def spatial_gating(u, v, ln_g, ln_b, ws, bs):
    u = jax.nn.gelu(u, approximate=False)
    v = layer_norm(jax.nn.gelu(v, approximate=False), ln_g, ln_b)
    bsz, s, _ = v.shape
    vh = v.reshape(bsz, s // SGU_CHUNK, SGU_CHUNK, SGU_HEADS, SGU_HD)
    mask = jnp.tril(jnp.ones((SGU_CHUNK, SGU_CHUNK), dtype=bool))
    wm = jnp.where(mask[None], ws, jnp.zeros((), ws.dtype))
    mixed = jnp.einsum('hts,bnshc->bnthc', wm, vh) + jnp.transpose(bs)[None, None, :, :, None]
    return u * mixed.reshape(bsz, s, D_C)


def conformer_conv(a, g, conv_w, conv_b, gn_g, gn_b):
    z = a * jax.nn.sigmoid(g)
    z = causal_dwconv(z, conv_w) + conv_b
    bsz, s, _ = z.shape
    z32 = z.astype(jnp.float32).reshape(bsz, s, CONF_GROUPS, CONF_GD)
    mu = jnp.mean(z32, axis=-1, keepdims=True)
    zc = z32 - mu
    zn = (zc * lax.rsqrt(jnp.mean(zc * zc, axis=-1, keepdims=True) + EPS)).reshape(bsz, s, D_D)
    zn = (zn * gn_g + gn_b).astype(z.dtype)
    return jax.nn.silu(zn)


def peer_ffn(xn, w_q, subkeys, u_tab, v_tab):
    bsz, s, d = xn.shape
    t = bsz * s
    xt = xn.reshape(t, d)
    q = (xt @ w_q).reshape(t, PEER_HEADS, 2, PEER_DKH)
    scores = jnp.einsum('thpk,hpnk->thpn', q.astype(jnp.float32), subkeys.astype(jnp.float32))
    s_top, i_top = lax.top_k(scores, PEER_TOPK)
    cand_s = s_top[:, :, 0, :, None] + s_top[:, :, 1, None, :]
    cand_i = i_top[:, :, 0, :, None] * PEER_NKEYS + i_top[:, :, 1, None, :]
    kk = PEER_TOPK * PEER_TOPK
    f_s, f_pos = lax.top_k(cand_s.reshape(t, PEER_HEADS, kk), PEER_TOPK)
    e_idx = jnp.take_along_axis(cand_i.reshape(t, PEER_HEADS, kk), f_pos, axis=-1)
    gates = jax.nn.softmax(f_s, axis=-1).astype(xn.dtype)
    n_sel = PEER_HEADS * PEER_TOPK
    n_blk = t // PEER_TOKEN_CHUNK

    def block_fn(args):
        xc, ic, gc = args
        uc = jnp.take(u_tab, ic, axis=0)
        act = gc * jax.nn.gelu(jnp.einsum('cd,ced->ce', xc, uc), approximate=False)
        vc = jnp.take(v_tab, ic, axis=0)
        return jnp.einsum('ce,ced->cd', act, vc)

    out = lax.map(block_fn, (xt.reshape(n_blk, PEER_TOKEN_CHUNK, d),
                             e_idx.reshape(n_blk, PEER_TOKEN_CHUNK, n_sel),
                             gates.reshape(n_blk, PEER_TOKEN_CHUNK, n_sel)))
    return out.reshape(bsz, s, d)


def setup_inputs(seed: int = 0) -> dict:
    key = jax.random.key(seed)
    ks = jax.random.split(key, 24)
    n_ev = (DEPTH + 1) // 2
    n_od = DEPTH // 2

    def nrm(k, shape, scale):
        return jax.random.normal(k, shape, jnp.float32) * scale

    return {
        'x': nrm(ks[0], (BATCH, SEQ, D_MODEL), 1.0),
        'mix_norm_g': 1.0 + nrm(ks[1], (DEPTH, D_MODEL), 0.1),
        'ffn_norm_g': 1.0 + nrm(ks[2], (DEPTH, D_MODEL), 0.1),
        'ev_w_in': nrm(ks[3], (n_ev, D_MODEL, D_A + 3 * D_B), D_MODEL ** -0.5),
        'ev_pool_w': nrm(ks[4], (n_ev, POOL_GROUPS, POOL_GC, POOL_GC), POOL_GC ** -0.5),
        'ev_pool_scale': 1.0 + nrm(ks[5], (n_ev, D_A), 0.1),
        'ev_conv_w': nrm(ks[6], (n_ev, SHORT_CONV, D_B), SHORT_CONV ** -0.5),
        'ev_w_out': nrm(ks[7], (n_ev, D_A + D_B, D_MODEL), (D_A + D_B) ** -0.5),
        'od_w_in': nrm(ks[8], (n_od, D_MODEL, 2 * D_C + 2 * D_D), D_MODEL ** -0.5),
        'od_v_norm_g': 1.0 + nrm(ks[9], (n_od, D_C), 0.1),
        'od_v_norm_b': nrm(ks[10], (n_od, D_C), 0.02),
        'od_sgu_w': nrm(ks[11], (n_od, SGU_HEADS, SGU_CHUNK, SGU_CHUNK), SGU_CHUNK ** -0.5),
        'od_sgu_b': 1.0 + nrm(ks[12], (n_od, SGU_HEADS, SGU_CHUNK), 0.1),
        'od_conv_w': nrm(ks[13], (n_od, CONF_KERNEL, D_D), CONF_KERNEL ** -0.5),
        'od_conv_b': nrm(ks[14], (n_od, D_D), 0.02),
        'od_gn_g': 1.0 + nrm(ks[15], (n_od, D_D), 0.1),
        'od_gn_b': nrm(ks[16], (n_od, D_D), 0.02),
        'od_w_out': nrm(ks[17], (n_od, D_C + D_D, D_MODEL), (D_C + D_D) ** -0.5),
        'peer_w_q': nrm(ks[18], (DEPTH, D_MODEL, PEER_HEADS * PEER_DK), D_MODEL ** -0.5),
        'peer_subkeys': nrm(ks[19], (DEPTH, PEER_HEADS, 2, PEER_NKEYS, PEER_DKH), PEER_DKH ** -0.5),
        'peer_u': nrm(ks[20], (DEPTH, PEER_EXPERTS, D_MODEL), D_MODEL ** -0.5),
        'peer_v': nrm(ks[21], (DEPTH, PEER_EXPERTS, D_MODEL), PEER_HEADS ** -0.5),
        'final_norm_g': 1.0 + nrm(ks[22], (D_MODEL,), 0.1),
    }


def reference(x, mix_norm_g, ffn_norm_g, ev_w_in, ev_pool_w, ev_pool_scale, ev_conv_w, ev_w_out,
              od_w_in, od_v_norm_g, od_v_norm_b, od_sgu_w, od_sgu_b, od_conv_w, od_conv_b,
              od_gn_g, od_gn_b, od_w_out, peer_w_q, peer_subkeys, peer_u, peer_v, final_norm_g):
    h = x
    for layer in range(DEPTH):
        j = layer // 2
        z = rms_norm(h, mix_norm_g[layer])
        if layer % 2 == 0:
            p = z @ ev_w_in[j]
            xa, bg, cg, hb = jnp.split(p, [D_A, D_A + D_B, D_A + 2 * D_B], axis=-1)
            ya = pool_mixer(xa, ev_pool_w[j], ev_pool_scale[j])
            yb = bg * causal_dwconv(cg * hb, ev_conv_w[j])
            mix = jnp.concatenate([ya, yb], axis=-1) @ ev_w_out[j]
        else:
            p = z @ od_w_in[j]
            u, v, a, g = jnp.split(p, [D_C, 2 * D_C, 2 * D_C + D_D], axis=-1)
            yc = spatial_gating(u, v, od_v_norm_g[j], od_v_norm_b[j], od_sgu_w[j], od_sgu_b[j])
            yd = conformer_conv(a, g, od_conv_w[j], od_conv_b[j], od_gn_g[j], od_gn_b[j])
            mix = jnp.concatenate([yc, yd], axis=-1) @ od_w_out[j]
        h = h + mix
        h = h + peer_ffn(rms_norm(h, ffn_norm_g[layer]), peer_w_q[layer], peer_subkeys[layer],
                         peer_u[layer], peer_v[layer])
    return rms_norm(h, final_norm_g)
```

```python
import functools

import jax
import jax.numpy as jnp
from jax import lax
from jax.experimental import pallas as pl
from jax.experimental.pallas import tpu as pltpu

F32 = jnp.float32
BF16 = jnp.bfloat16
EPS = 1e-6

V7X_VMEM_BYTES = 64 * 1024 * 1024
V7X_LANES = 128
V7X_SUBLANES = 8
COMPILER_VMEM_ALLOWANCE = 12 * 1024 * 1024

POOL_WINDOWS = (2, 4, 8, 16)
SHORT_CONV = 3
SGU_CHUNK = 128
SGU_HEADS = 8
CONF_KERNEL = 31
CONF_GROUPS = 8
PEER_HEADS = 8
PEER_NKEYS = 128
KEY_BITS = 7
PEER_TOPK = 16
PEER_DKH = 64
N_SEL = PEER_HEADS * PEER_TOPK
MIX0_HALO = 16
MIX1_HALO = 32
G_PITCH = 136


def _params(semantics, *buffer_bytes):
    limit = min(int(sum(buffer_bytes)) + COMPILER_VMEM_ALLOWANCE, V7X_VMEM_BYTES - 4 * 1024 * 1024)
    return pltpu.CompilerParams(dimension_semantics=semantics, vmem_limit_bytes=limit)


def _nbytes(shape, dtype, buffers=2):
    n = 1
    for s in shape:
        n *= s
    return n * jnp.dtype(dtype).itemsize * buffers


def _gelu(x):
    return 0.5 * x * (1.0 + lax.erf(x * (2.0 ** -0.5)))


def _tile(total, want):
    t = min(total, want)
    assert total % t == 0, (total, want)
    return t


def _norm_mm_kernel(x_ref, g_ref, w_ref, o_ref, zn_ref):
    @pl.when(pl.program_id(1) == 0)
    def _():
        x = x_ref[...]
        ms = jnp.mean(x * x, axis=-1, keepdims=True)
        zn_ref[...] = (x * lax.rsqrt(ms + EPS) * g_ref[...]).astype(BF16)

    o_ref[...] = jnp.dot(zn_ref[...], w_ref[...], preferred_element_type=F32)


def _norm_matmul(x, g, w):
    t, d = x.shape
    n = w.shape[1]
    tb, tn = _tile(t, 1024), _tile(n, 1024)
    return pl.pallas_call(
        _norm_mm_kernel,
        out_shape=jax.ShapeDtypeStruct((t, n), F32),
        grid=(t // tb, n // tn),
        in_specs=[pl.BlockSpec((tb, d), lambda i, j: (i, 0)),
                  pl.BlockSpec((1, d), lambda i, j: (0, 0)),
                  pl.BlockSpec((d, tn), lambda i, j: (0, j))],
        out_specs=pl.BlockSpec((tb, tn), lambda i, j: (i, j)),
        scratch_shapes=[pltpu.VMEM((tb, d), BF16)],
        compiler_params=_params(("parallel", "arbitrary"), _nbytes((tb, d), F32), _nbytes((d, tn), BF16),
                                _nbytes((tb, tn), F32), _nbytes((tb, d), BF16, 1)),
        name="norm_matmul",
    )(x, g.reshape(1, d), w)


def _mm_res_kernel(y_ref, w_ref, h_ref, o_ref):
    o_ref[...] = h_ref[...] + jnp.dot(y_ref[...], w_ref[...], preferred_element_type=F32)


def _matmul_residual(y, w, h):
    t, k = y.shape
    n = w.shape[1]
    tb, tn = _tile(t, 1024), _tile(n, 1024)
    return pl.pallas_call(
        _mm_res_kernel,
        out_shape=jax.ShapeDtypeStruct((t, n), F32),
        grid=(t // tb, n // tn),
        in_specs=[pl.BlockSpec((tb, k), lambda i, j: (i, 0)),
                  pl.BlockSpec((k, tn), lambda i, j: (0, j)),
                  pl.BlockSpec((tb, tn), lambda i, j: (i, j))],
        out_specs=pl.BlockSpec((tb, tn), lambda i, j: (i, j)),
        compiler_params=_params(("parallel", "parallel"), _nbytes((tb, k), BF16), _nbytes((k, tn), BF16),
                                _nbytes((tb, tn), F32), _nbytes((tb, tn), F32)),
        name="matmul_residual",
    )(y, w, h)


def _mix0_kernel(p_ref, halo_ref, pw_ref, ps_ref, cw_ref, y_ref, ext_ref, *, tbm, seq, d_a):
    row0 = pl.program_id(0) * tbm
    at_seq_start = (row0 % seq) == 0
    halo = MIX0_HALO
    gc = d_a // len(POOL_WINDOWS)

    ext_ref[0:halo, 0:d_a] = jnp.where(at_seq_start, 0.0, halo_ref[:, 0:d_a])
    ext_ref[0:halo, d_a:2 * d_a] = jnp.where(
        at_seq_start, 0.0, halo_ref[:, 2 * d_a:3 * d_a] * halo_ref[:, 3 * d_a:4 * d_a])
    ext_ref[halo:halo + tbm, 0:d_a] = p_ref[:, 0:d_a]
    ext_ref[halo:halo + tbm, d_a:2 * d_a] = p_ref[:, 2 * d_a:3 * d_a] * p_ref[:, 3 * d_a:4 * d_a]

    pos = (row0 % seq) + lax.broadcasted_iota(jnp.int32, (tbm, gc), 0)
    for g, win in enumerate(POOL_WINDOWS):
        cs = slice(g * gc, (g + 1) * gc)
        x = ext_ref[halo:halo + tbm, cs]
        s = x
        for dlt in range(1, win):
            s = s + ext_ref[halo - dlt:halo - dlt + tbm, cs]
        cnt = jnp.minimum(pos + 1, win).astype(F32)
        pooled = s / cnt - x
        mixed = jnp.dot(pooled.astype(BF16), pw_ref[g], preferred_element_type=F32)
        y_ref[:, cs] = (mixed * ps_ref[:, cs]).astype(BF16)

    for c0 in range(0, d_a, 4 * V7X_LANES):
        cs = slice(c0, c0 + 4 * V7X_LANES)
        es = slice(d_a + c0, d_a + c0 + 4 * V7X_LANES)
        conv = cw_ref[SHORT_CONV - 1:SHORT_CONV, cs] * ext_ref[halo:halo + tbm, es]
        for dlt in range(1, SHORT_CONV):
            k = SHORT_CONV - 1 - dlt
            conv = conv + cw_ref[k:k + 1, cs] * ext_ref[halo - dlt:halo - dlt + tbm, es]
        bg = p_ref[:, d_a + c0:d_a + c0 + 4 * V7X_LANES]
        y_ref[:, d_a + c0:d_a + c0 + 4 * V7X_LANES] = (bg * conv).astype(BF16)


def _mix0(p, pool_w, pool_scale, conv_w, *, seq):
    t, four_da = p.shape
    d_a = four_da // 4
    tbm = _tile(seq, 256)
    halo = MIX0_HALO
    gc = d_a // len(POOL_WINDOWS)
    kern = functools.partial(_mix0_kernel, tbm=tbm, seq=seq, d_a=d_a)
    return pl.pallas_call(
        kern,
        out_shape=jax.ShapeDtypeStruct((t, 2 * d_a), BF16),
        grid=(t // tbm,),
        in_specs=[pl.BlockSpec((tbm, four_da), lambda i: (i, 0)),
                  pl.BlockSpec((halo, four_da), lambda i: (jnp.maximum(i * (tbm // halo) - 1, 0), 0)),
                  pl.BlockSpec((len(POOL_WINDOWS), gc, gc), lambda i: (0, 0, 0)),
                  pl.BlockSpec((1, d_a), lambda i: (0, 0)),
                  pl.BlockSpec((SHORT_CONV, d_a), lambda i: (0, 0))],
        out_specs=pl.BlockSpec((tbm, 2 * d_a), lambda i: (i, 0)),
        scratch_shapes=[pltpu.VMEM((halo + tbm, 2 * d_a), F32)],
        compiler_params=_params(("parallel",), _nbytes((tbm, four_da), F32), _nbytes((halo, four_da), F32),
                                _nbytes((tbm, 2 * d_a), BF16), _nbytes((halo + tbm, 2 * d_a), F32, 1)),
        name="mix_even",
    )(p, p, pool_w, pool_scale.reshape(1, d_a), conv_w)


def _mix1_kernel(p_ref, halo_ref, lng_ref, lnb_ref, ws_ref, sb_ref, cw_ref, cb_ref, gng_ref, gnb_ref,
                 y_ref, vn_ref, zext_ref, *, tbm, seq, d_c):
    row0 = pl.program_id(0) * tbm
    at_seq_start = (row0 % seq) == 0
    halo = MIX1_HALO
    hd = d_c // SGU_HEADS

    v = _gelu(p_ref[:, d_c:2 * d_c])
    mu = jnp.mean(v, axis=-1, keepdims=True)
    vc = v - mu
    var = jnp.mean(vc * vc, axis=-1, keepdims=True)
    vn_ref[...] = ((vc * lax.rsqrt(var + EPS)) * lng_ref[...] + lnb_ref[...]).astype(BF16)

    r = lax.broadcasted_iota(jnp.int32, (SGU_CHUNK, SGU_CHUNK), 0)
    c = lax.broadcasted_iota(jnp.int32, (SGU_CHUNK, SGU_CHUNK), 1)
    for h in range(SGU_HEADS):
        cs = slice(h * hd, (h + 1) * hd)
        wm = jnp.where(r >= c, ws_ref[h], 0.0).astype(BF16)
        for ch in range(tbm // SGU_CHUNK):
            rs = slice(ch * SGU_CHUNK, (ch + 1) * SGU_CHUNK)
            mixed = jnp.dot(wm, vn_ref[rs, cs], preferred_element_type=F32) + sb_ref[:, cs]
            u = _gelu(p_ref[rs, cs])
            y_ref[rs, cs] = (u * mixed).astype(BF16)

    zext_ref[0:halo, :] = jnp.where(
        at_seq_start, 0.0, halo_ref[:, 2 * d_c:3 * d_c] * jax.nn.sigmoid(halo_ref[:, 3 * d_c:4 * d_c]))
    zext_ref[halo:halo + tbm, :] = p_ref[:, 2 * d_c:3 * d_c] * jax.nn.sigmoid(p_ref[:, 3 * d_c:4 * d_c])
    gd = d_c // CONF_GROUPS
    first = halo - (CONF_KERNEL - 1)
    for gi in range(CONF_GROUPS):
        cs = slice(gi * gd, (gi + 1) * gd)
        acc = cb_ref[:, cs] + cw_ref[0:1, cs] * zext_ref[first:first + tbm, cs]
        for k in range(1, CONF_KERNEL):
            acc = acc + cw_ref[k:k + 1, cs] * zext_ref[first + k:first + k + tbm, cs]
        mu = jnp.mean(acc, axis=-1, keepdims=True)
        zc = acc - mu
        var = jnp.mean(zc * zc, axis=-1, keepdims=True)
        zn = (zc * lax.rsqrt(var + EPS)) * gng_ref[:, cs] + gnb_ref[:, cs]
        y_ref[:, d_c + gi * gd:d_c + (gi + 1) * gd] = (zn * jax.nn.sigmoid(zn)).astype(BF16)


def _mix1(p, ln_g, ln_b, sgu_w, sgu_b, conv_w, conv_b, gn_g, gn_b, *, seq):
    t, four_dc = p.shape
    d_c = four_dc // 4
    tbm = _tile(seq, 256)
    halo = MIX1_HALO
    hd = d_c // SGU_HEADS
    sb = jnp.repeat(sgu_b.T, hd, axis=1)
    row = lambda a: a.reshape(1, d_c)
    kern = functools.partial(_mix1_kernel, tbm=tbm, seq=seq, d_c=d_c)
    const2 = lambda i: (0, 0)
    return pl.pallas_call(
        kern,
        out_shape=jax.ShapeDtypeStruct((t, 2 * d_c), BF16),
        grid=(t // tbm,),
        in_specs=[pl.BlockSpec((tbm, four_dc), lambda i: (i, 0)),
                  pl.BlockSpec((halo, four_dc), lambda i: (jnp.maximum(i * (tbm // halo) - 1, 0), 0)),
                  pl.BlockSpec((1, d_c), const2),
                  pl.BlockSpec((1, d_c), const2),
                  pl.BlockSpec((SGU_HEADS, SGU_CHUNK, SGU_CHUNK), lambda i: (0, 0, 0)),
                  pl.BlockSpec((SGU_CHUNK, d_c), const2),
                  pl.BlockSpec((CONF_KERNEL, d_c), const2),
                  pl.BlockSpec((1, d_c), const2),
                  pl.BlockSpec((1, d_c), const2),
                  pl.BlockSpec((1, d_c), const2)],
        out_specs=pl.BlockSpec((tbm, 2 * d_c), lambda i: (i, 0)),
        scratch_shapes=[pltpu.VMEM((tbm, d_c), BF16), pltpu.VMEM((halo + tbm, d_c), F32)],
        compiler_params=_params(("parallel",), _nbytes((tbm, four_dc), F32), _nbytes((halo, four_dc), F32),
                                _nbytes((tbm, 2 * d_c), BF16), _nbytes((halo + tbm, d_c), F32, 1),
                                _nbytes((SGU_CHUNK, d_c), F32)),
        name="mix_odd",
    )(p, p, row(ln_g), row(ln_b), sgu_w, sb, conv_w, row(conv_b), row(gn_g), row(gn_b))


def _top16_rows(x, key_id):
    vals, ids = [], []
    for _ in range(PEER_TOPK):
        m = jnp.max(x, axis=0, keepdims=True)
        sel = jnp.min(jnp.where(x == m, key_id, float(2 ** 20)), axis=0, keepdims=True)
        x = jnp.where(key_id == sel, -jnp.inf, x)
        vals.append(m)
        ids.append(sel)
    return vals, ids


def _select_kernel(h_ref, g_ref, wq_ref, keys_ref, xn_ref, e_ref, gate_ref, qt_ref, fs_ref, fe_ref, *, tbs):
    x = h_ref[...]
    ms = jnp.mean(x * x, axis=-1, keepdims=True)
    xn = (x * lax.rsqrt(ms + EPS) * g_ref[...]).astype(BF16)
    xn_ref[...] = xn
    qt_ref[...] = lax.dot_general(wq_ref[...], xn, (((1,), (1,)), ((), ())),
                                  preferred_element_type=F32).astype(BF16)

    key_id = lax.broadcasted_iota(jnp.int32, (PEER_NKEYS, tbs), 0).astype(F32)
    sub = lax.broadcasted_iota(jnp.int32, (V7X_SUBLANES, tbs), 0).astype(F32)
    half = PEER_TOPK // 2

    def head(hd, carry):
        tops = []
        for part in range(2):
            hp = hd * 2 + part
            q = qt_ref[pl.ds(pl.multiple_of(hp * PEER_DKH, PEER_DKH), PEER_DKH), :]
            scores = jnp.dot(keys_ref[hp], q, preferred_element_type=F32)
            tops.append(_top16_rows(scores, key_id))
        (s0, i0), (s1, i1) = tops
        s1_lo, s1_hi = jnp.concatenate(s1[:half], axis=0), jnp.concatenate(s1[half:], axis=0)
        i1_lo, i1_hi = jnp.concatenate(i1[:half], axis=0), jnp.concatenate(i1[half:], axis=0)
        cv, ce, cf = [], [], []
        for k in range(half):
            cv.append(s0[k] + s1_lo)
            ce.append(i0[k] * float(PEER_NKEYS) + i1_lo)
            cf.append(sub + float(k * PEER_TOPK))
        cv.append(s0[0] + s1_hi)
        ce.append(i0[0] * float(PEER_NKEYS) + i1_hi)
        cf.append(sub + float(half))
        cv.append(jnp.concatenate(s0[half:], axis=0) + s1[0])
        ce.append(jnp.concatenate(i0[half:], axis=0) * float(PEER_NKEYS) + i1[0])
        cf.append((sub + float(half)) * float(PEER_TOPK))
        cv, ce, cf = (jnp.concatenate(a, axis=0) for a in (cv, ce, cf))
        f_s, f_e = [], []
        for _ in range(PEER_TOPK):
            m = jnp.max(cv, axis=0, keepdims=True)
            pick = jnp.min(jnp.where(cv == m, cf, float(2 ** 20)), axis=0, keepdims=True)
            hit = cf == pick
            f_e.append(jnp.max(jnp.where(hit, ce, -1.0), axis=0, keepdims=True))
            cv = jnp.where(hit, -jnp.inf, cv)
            f_s.append(m)
        f_s = jnp.concatenate(f_s, axis=0)
        ex = jnp.exp(f_s - f_s[0:1])
        gates = ex / jnp.sum(ex, axis=0, keepdims=True)
        rows = pl.ds(pl.multiple_of(hd * PEER_TOPK, PEER_TOPK), PEER_TOPK)
        fs_ref[rows, :] = gates
        fe_ref[rows, :] = jnp.concatenate(f_e, axis=0)
        return carry

    lax.fori_loop(0, PEER_HEADS, head, 0)
    gate_ref[...] = fs_ref[...].T
    e_ref[...] = fe_ref[...].T


def _select(h, g, wq_t, keys):
    t, d = h.shape
    tbs = _tile(t, 256)
    kern = functools.partial(_select_kernel, tbs=tbs)
    nq = wq_t.shape[0]
    return pl.pallas_call(
        kern,
        out_shape=(jax.ShapeDtypeStruct((t, d), BF16),
                   jax.ShapeDtypeStruct((t, N_SEL), F32),
                   jax.ShapeDtypeStruct((t, N_SEL), F32)),
        grid=(t // tbs,),
        in_specs=[pl.BlockSpec((tbs, d), lambda i: (i, 0)),
                  pl.BlockSpec((1, d), lambda i: (0, 0)),
                  pl.BlockSpec((nq, d), lambda i: (0, 0)),
                  pl.BlockSpec(keys.shape, lambda i: (0, 0, 0))],
        out_specs=(pl.BlockSpec((tbs, d), lambda i: (i, 0)),
                   pl.BlockSpec((tbs, N_SEL), lambda i: (i, 0)),
                   pl.BlockSpec((tbs, N_SEL), lambda i: (i, 0))),
        scratch_shapes=[pltpu.VMEM((nq, tbs), BF16),
                        pltpu.VMEM((N_SEL, tbs), F32),
                        pltpu.VMEM((N_SEL, tbs), F32)],
        compiler_params=_params(("parallel",), _nbytes((tbs, d), F32), _nbytes((nq, d), BF16),
                                _nbytes((tbs, d), BF16), _nbytes(keys.shape, BF16)),
        name="peer_select",
    )(h, g.reshape(1, d), wq_t, keys)


def _gates_kernel(e_ref, gate_ref, o_ref, tile_ref, *, tbg):
    sub_id = lax.broadcasted_iota(jnp.int32, (PEER_NKEYS, N_SEL), 0)

    def token(tk, carry):
        e = e_ref[pl.ds(tk, 1), :].astype(jnp.int32)
        gate = gate_ref[pl.ds(tk, 1), :]
        hot_i = jnp.where(sub_id == (e >> KEY_BITS), 1.0, 0.0).astype(BF16)
        hot_j = jnp.where(sub_id == (e & (PEER_NKEYS - 1)), gate, 0.0).astype(BF16)
        tile = lax.dot_general(hot_i, hot_j, (((1,), (1,)), ((), ())), preferred_element_type=F32)
        tile_ref[pl.ds(pl.multiple_of(tk * G_PITCH, V7X_SUBLANES), PEER_NKEYS), :] = tile
        return carry

    lax.fori_loop(0, tbg, token, 0)
    for i in range(PEER_NKEYS):
        o_ref[:, i * PEER_NKEYS:(i + 1) * PEER_NKEYS] = (
            tile_ref[pl.ds(i, tbg, stride=G_PITCH), :].astype(BF16))


def _dense_gates(e, gate):
    t = e.shape[0]
    n_exp = PEER_NKEYS * PEER_NKEYS
    tbg = _tile(t, 128)
    kern = functools.partial(_gates_kernel, tbg=tbg)
    return pl.pallas_call(
        kern,
        out_shape=jax.ShapeDtypeStruct((t, n_exp), BF16),
        grid=(t // tbg,),
        in_specs=[pl.BlockSpec((tbg, N_SEL), lambda i: (i, 0)),
                  pl.BlockSpec((tbg, N_SEL), lambda i: (i, 0))],
        out_specs=pl.BlockSpec((tbg, n_exp), lambda i: (i, 0)),
        scratch_shapes=[pltpu.VMEM((tbg * G_PITCH, PEER_NKEYS), F32)],
        compiler_params=_params(("parallel",), _nbytes((tbg, n_exp), BF16),
                                _nbytes((tbg * G_PITCH, PEER_NKEYS), F32, 1)),
        name="peer_dense_gates",
    )(e, gate)


def _ffn_kernel(xn_ref, u_ref, v_ref, gt_ref, h_ref, fg_ref, o_ref, *, final_norm):
    n = pl.program_id(1)

    @pl.when(n == 0)
    def _():
        o_ref[...] = h_ref[...]

    s = lax.dot_general(xn_ref[...], u_ref[...], (((1,), (1,)), ((), ())), preferred_element_type=F32)
    act = (_gelu(s) * gt_ref[...].astype(F32)).astype(BF16)
    o_ref[...] += jnp.dot(act, v_ref[...], preferred_element_type=F32)

    if final_norm:
        @pl.when(n == pl.num_programs(1) - 1)
        def _():
            x = o_ref[...]
            ms = jnp.mean(x * x, axis=-1, keepdims=True)
            o_ref[...] = x * lax.rsqrt(ms + EPS) * fg_ref[...]


def _peer_ffn(xn, u, v, gt, h, final_g, *, final_norm):
    t, d = xn.shape
    n_exp = u.shape[0]
    tb, nb = _tile(t, 1024), _tile(n_exp, 512)
    kern = functools.partial(_ffn_kernel, final_norm=final_norm)
    return pl.pallas_call(
        kern,
        out_shape=jax.ShapeDtypeStruct((t, d), F32),
        grid=(t // tb, n_exp // nb),
        in_specs=[pl.BlockSpec((tb, d), lambda i, n: (i, 0)),
                  pl.BlockSpec((nb, d), lambda i, n: (n, 0)),
                  pl.BlockSpec((nb, d), lambda i, n: (n, 0)),
                  pl.BlockSpec((tb, nb), lambda i, n: (i, n)),
                  pl.BlockSpec((tb, d), lambda i, n: (i, 0), pipeline_mode=pl.Buffered(1)),
                  pl.BlockSpec((1, d), lambda i, n: (0, 0))],
        out_specs=pl.BlockSpec((tb, d), lambda i, n: (i, 0)),
        compiler_params=_params(("parallel", "arbitrary"), _nbytes((tb, d), BF16), 2 * _nbytes((nb, d), BF16),
                                _nbytes((tb, nb), BF16), _nbytes((tb, d), F32, 1), _nbytes((tb, d), F32),
                                _nbytes((tb, nb), F32, 1)),
        name="peer_ffn",
    )(xn, u, v, gt, h, final_g.reshape(1, d))


def kernel(x, mix_norm_g, ffn_norm_g, ev_w_in, ev_pool_w, ev_pool_scale, ev_conv_w, ev_w_out, od_w_in,
           od_v_norm_g, od_v_norm_b, od_sgu_w, od_sgu_b, od_conv_w, od_conv_b, od_gn_g, od_gn_b, od_w_out,
           peer_w_q, peer_subkeys, peer_u, peer_v, final_norm_g):
    bsz, seq, d = x.shape
    depth = mix_norm_g.shape[0]
    h = x.reshape(bsz * seq, d)
    for layer in range(depth):
        j = layer // 2
        if layer % 2 == 0:
            p = _norm_matmul(h, mix_norm_g[layer], ev_w_in[j].astype(BF16))
            y = _mix0(p, ev_pool_w[j].astype(BF16), ev_pool_scale[j], ev_conv_w[j], seq=seq)
            h = _matmul_residual(y, ev_w_out[j].astype(BF16), h)
        else:
            p = _norm_matmul(h, mix_norm_g[layer], od_w_in[j].astype(BF16))
            y = _mix1(p, od_v_norm_g[j], od_v_norm_b[j], od_sgu_w[j], od_sgu_b[j], od_conv_w[j],
                      od_conv_b[j], od_gn_g[j], od_gn_b[j], seq=seq)
            h = _matmul_residual(y, od_w_out[j].astype(BF16), h)
        keys = peer_subkeys[layer].reshape(2 * PEER_HEADS, PEER_NKEYS, PEER_DKH).astype(BF16)
        xn, e, gate = _select(h, ffn_norm_g[layer], peer_w_q[layer].T.astype(BF16), keys)
        gt = _dense_gates(e, gate)
        h = _peer_ffn(xn, peer_u[layer].astype(BF16), peer_v[layer].astype(BF16), gt, h, final_norm_g,
                      final_norm=(layer == depth - 1))
    return h.reshape(bsz, seq, d)
```

```python
import functools

import jax
import jax.numpy as jnp
from jax import lax
from jax.experimental import pallas as pl
from jax.experimental.pallas import tpu as pltpu

F32 = jnp.float32
BF16 = jnp.bfloat16
EPS = 1e-6

V7X_VMEM_BYTES = 64 * 1024 * 1024
V7X_LANES = 128
V7X_SUBLANES = 8
COMPILER_VMEM_ALLOWANCE = 12 * 1024 * 1024

POOL_WINDOWS = (2, 4, 8, 16)
SHORT_CONV = 3
SGU_CHUNK = 128
SGU_HEADS = 8
CONF_KERNEL = 31
CONF_GROUPS = 8
PEER_HEADS = 8
PEER_NKEYS = 128
KEY_BITS = 7
PEER_TOPK = 16
PEER_DKH = 64
N_SEL = PEER_HEADS * PEER_TOPK
MIX0_HALO = 16
MIX1_HALO = 32
TOKEN_UNROLL = 32


def _params(semantics, *buffer_bytes):
    limit = min(int(sum(buffer_bytes)) + COMPILER_VMEM_ALLOWANCE, V7X_VMEM_BYTES - 4 * 1024 * 1024)
    return pltpu.CompilerParams(dimension_semantics=semantics, vmem_limit_bytes=limit)


def _nbytes(shape, dtype, buffers=2):
    n = 1
    for s in shape:
        n *= s
    return n * jnp.dtype(dtype).itemsize * buffers


def _gelu(x):
    return 0.5 * x * (1.0 + lax.erf(x * (2.0 ** -0.5)))


def _tile(total, want):
    t = min(total, want)
    assert total % t == 0, (total, want)
    return t


def _norm_mm_kernel(x_ref, g_ref, w_ref, o_ref, zn_ref):
    @pl.when(pl.program_id(1) == 0)
    def _():
        x = x_ref[...]
        ms = jnp.mean(x * x, axis=-1, keepdims=True)
        zn_ref[...] = (x * lax.rsqrt(ms + EPS) * g_ref[...]).astype(BF16)

    o_ref[...] = jnp.dot(zn_ref[...], w_ref[...], preferred_element_type=F32)


def _norm_matmul(x, g, w):
    t, d = x.shape
    n = w.shape[1]
    tb, tn = _tile(t, 1024), _tile(n, 1024)
    return pl.pallas_call(
        _norm_mm_kernel,
        out_shape=jax.ShapeDtypeStruct((t, n), F32),
        grid=(t // tb, n // tn),
        in_specs=[pl.BlockSpec((tb, d), lambda i, j: (i, 0)),
                  pl.BlockSpec((1, d), lambda i, j: (0, 0)),
                  pl.BlockSpec((d, tn), lambda i, j: (0, j))],
        out_specs=pl.BlockSpec((tb, tn), lambda i, j: (i, j)),
        scratch_shapes=[pltpu.VMEM((tb, d), BF16)],
        compiler_params=_params(("parallel", "arbitrary"), _nbytes((tb, d), F32), _nbytes((d, tn), BF16),
                                _nbytes((tb, tn), F32), _nbytes((tb, d), BF16, 1)),
        name="norm_matmul",
    )(x, g.reshape(1, d), w)


def _mm_res_kernel(y_ref, w_ref, h_ref, o_ref):
    o_ref[...] = h_ref[...] + jnp.dot(y_ref[...], w_ref[...], preferred_element_type=F32)


def _matmul_residual(y, w, h):
    t, k = y.shape
    n = w.shape[1]
    tb, tn = _tile(t, 1024), _tile(n, 1024)
    return pl.pallas_call(
        _mm_res_kernel,
        out_shape=jax.ShapeDtypeStruct((t, n), F32),
        grid=(t // tb, n // tn),
        in_specs=[pl.BlockSpec((tb, k), lambda i, j: (i, 0)),
                  pl.BlockSpec((k, tn), lambda i, j: (0, j)),
                  pl.BlockSpec((tb, tn), lambda i, j: (i, j))],
        out_specs=pl.BlockSpec((tb, tn), lambda i, j: (i, j)),
        compiler_params=_params(("parallel", "parallel"), _nbytes((tb, k), BF16), _nbytes((k, tn), BF16),
                                _nbytes((tb, tn), F32), _nbytes((tb, tn), F32)),
        name="matmul_residual",
    )(y, w, h)


def _mix0_kernel(p_ref, halo_ref, pw_ref, ps_ref, cw_ref, y_ref, ext_ref, *, tbm, seq, d_a):
    row0 = pl.program_id(0) * tbm
    at_seq_start = (row0 % seq) == 0
    halo = MIX0_HALO
    gc = d_a // len(POOL_WINDOWS)

    ext_ref[0:halo, 0:d_a] = jnp.where(at_seq_start, 0.0, halo_ref[:, 0:d_a])
    ext_ref[0:halo, d_a:2 * d_a] = jnp.where(
        at_seq_start, 0.0, halo_ref[:, 2 * d_a:3 * d_a] * halo_ref[:, 3 * d_a:4 * d_a])
    ext_ref[halo:halo + tbm, 0:d_a] = p_ref[:, 0:d_a]
    ext_ref[halo:halo + tbm, d_a:2 * d_a] = p_ref[:, 2 * d_a:3 * d_a] * p_ref[:, 3 * d_a:4 * d_a]

    pos = (row0 % seq) + lax.broadcasted_iota(jnp.int32, (tbm, gc), 0)
    for g, win in enumerate(POOL_WINDOWS):
        cs = slice(g * gc, (g + 1) * gc)
        x = ext_ref[halo:halo + tbm, cs]
        s = x
        for dlt in range(1, win):
            s = s + ext_ref[halo - dlt:halo - dlt + tbm, cs]
        cnt = jnp.minimum(pos + 1, win).astype(F32)
        pooled = s / cnt - x
        mixed = jnp.dot(pooled.astype(BF16), pw_ref[g], preferred_element_type=F32)
        y_ref[:, cs] = (mixed * ps_ref[:, cs]).astype(BF16)

    for c0 in range(0, d_a, 4 * V7X_LANES):
        cs = slice(c0, c0 + 4 * V7X_LANES)
        es = slice(d_a + c0, d_a + c0 + 4 * V7X_LANES)
        conv = cw_ref[SHORT_CONV - 1:SHORT_CONV, cs] * ext_ref[halo:halo + tbm, es]
        for dlt in range(1, SHORT_CONV):
            k = SHORT_CONV - 1 - dlt
            conv = conv + cw_ref[k:k + 1, cs] * ext_ref[halo - dlt:halo - dlt + tbm, es]
        bg = p_ref[:, d_a + c0:d_a + c0 + 4 * V7X_LANES]
        y_ref[:, d_a + c0:d_a + c0 + 4 * V7X_LANES] = (bg * conv).astype(BF16)


def _mix0(p, pool_w, pool_scale, conv_w, *, seq):
    t, four_da = p.shape
    d_a = four_da // 4
    tbm = _tile(seq, 256)
    halo = MIX0_HALO
    gc = d_a // len(POOL_WINDOWS)
    kern = functools.partial(_mix0_kernel, tbm=tbm, seq=seq, d_a=d_a)
    return pl.pallas_call(
        kern,
        out_shape=jax.ShapeDtypeStruct((t, 2 * d_a), BF16),
        grid=(t // tbm,),
        in_specs=[pl.BlockSpec((tbm, four_da), lambda i: (i, 0)),
                  pl.BlockSpec((halo, four_da), lambda i: (jnp.maximum(i * (tbm // halo) - 1, 0), 0)),
                  pl.BlockSpec((len(POOL_WINDOWS), gc, gc), lambda i: (0, 0, 0)),
                  pl.BlockSpec((1, d_a), lambda i: (0, 0)),
                  pl.BlockSpec((SHORT_CONV, d_a), lambda i: (0, 0))],
        out_specs=pl.BlockSpec((tbm, 2 * d_a), lambda i: (i, 0)),
        scratch_shapes=[pltpu.VMEM((halo + tbm, 2 * d_a), F32)],
        compiler_params=_params(("parallel",), _nbytes((tbm, four_da), F32), _nbytes((halo, four_da), F32),
                                _nbytes((tbm, 2 * d_a), BF16), _nbytes((halo + tbm, 2 * d_a), F32, 1)),
        name="mix_even",
    )(p, p, pool_w, pool_scale.reshape(1, d_a), conv_w)


def _mix1_kernel(p_ref, halo_ref, lng_ref, lnb_ref, ws_ref, sb_ref, cw_ref, cb_ref, gng_ref, gnb_ref,
                 y_ref, vn_ref, zext_ref, *, tbm, seq, d_c):
    row0 = pl.program_id(0) * tbm
    at_seq_start = (row0 % seq) == 0
    halo = MIX1_HALO
    hd = d_c // SGU_HEADS

    v = _gelu(p_ref[:, d_c:2 * d_c])
    mu = jnp.mean(v, axis=-1, keepdims=True)
    vc = v - mu
    var = jnp.mean(vc * vc, axis=-1, keepdims=True)
    vn_ref[...] = ((vc * lax.rsqrt(var + EPS)) * lng_ref[...] + lnb_ref[...]).astype(BF16)

    r = lax.broadcasted_iota(jnp.int32, (SGU_CHUNK, SGU_CHUNK), 0)
    c = lax.broadcasted_iota(jnp.int32, (SGU_CHUNK, SGU_CHUNK), 1)
    for h in range(SGU_HEADS):
        cs = slice(h * hd, (h + 1) * hd)
        wm = jnp.where(r >= c, ws_ref[h], 0.0).astype(BF16)
        for ch in range(tbm // SGU_CHUNK):
            rs = slice(ch * SGU_CHUNK, (ch + 1) * SGU_CHUNK)
            mixed = jnp.dot(wm, vn_ref[rs, cs], preferred_element_type=F32) + sb_ref[:, cs]
            u = _gelu(p_ref[rs, cs])
            y_ref[rs, cs] = (u * mixed).astype(BF16)

    zext_ref[0:halo, :] = jnp.where(
        at_seq_start, 0.0, halo_ref[:, 2 * d_c:3 * d_c] * jax.nn.sigmoid(halo_ref[:, 3 * d_c:4 * d_c]))
    zext_ref[halo:halo + tbm, :] = p_ref[:, 2 * d_c:3 * d_c] * jax.nn.sigmoid(p_ref[:, 3 * d_c:4 * d_c])
    gd = d_c // CONF_GROUPS
    first = halo - (CONF_KERNEL - 1)
    for gi in range(CONF_GROUPS):
        cs = slice(gi * gd, (gi + 1) * gd)
        acc = cb_ref[:, cs] + cw_ref[0:1, cs] * zext_ref[first:first + tbm, cs]
        for k in range(1, CONF_KERNEL):
            acc = acc + cw_ref[k:k + 1, cs] * zext_ref[first + k:first + k + tbm, cs]
        mu = jnp.mean(acc, axis=-1, keepdims=True)
        zc = acc - mu
        var = jnp.mean(zc * zc, axis=-1, keepdims=True)
        zn = (zc * lax.rsqrt(var + EPS)) * gng_ref[:, cs] + gnb_ref[:, cs]
        y_ref[:, d_c + gi * gd:d_c + (gi + 1) * gd] = (zn * jax.nn.sigmoid(zn)).astype(BF16)


def _mix1(p, ln_g, ln_b, sgu_w, sgu_b, conv_w, conv_b, gn_g, gn_b, *, seq):
    t, four_dc = p.shape
    d_c = four_dc // 4
    tbm = _tile(seq, 256)
    halo = MIX1_HALO
    hd = d_c // SGU_HEADS
    sb = jnp.repeat(sgu_b.T, hd, axis=1)
    row = lambda a: a.reshape(1, d_c)
    kern = functools.partial(_mix1_kernel, tbm=tbm, seq=seq, d_c=d_c)
    const2 = lambda i: (0, 0)
    return pl.pallas_call(
        kern,
        out_shape=jax.ShapeDtypeStruct((t, 2 * d_c), BF16),
        grid=(t // tbm,),
        in_specs=[pl.BlockSpec((tbm, four_dc), lambda i: (i, 0)),
                  pl.BlockSpec((halo, four_dc), lambda i: (jnp.maximum(i * (tbm // halo) - 1, 0), 0)),
                  pl.BlockSpec((1, d_c), const2),
                  pl.BlockSpec((1, d_c), const2),
                  pl.BlockSpec((SGU_HEADS, SGU_CHUNK, SGU_CHUNK), lambda i: (0, 0, 0)),
                  pl.BlockSpec((SGU_CHUNK, d_c), const2),
                  pl.BlockSpec((CONF_KERNEL, d_c), const2),
                  pl.BlockSpec((1, d_c), const2),
                  pl.BlockSpec((1, d_c), const2),
                  pl.BlockSpec((1, d_c), const2)],
        out_specs=pl.BlockSpec((tbm, 2 * d_c), lambda i: (i, 0)),
        scratch_shapes=[pltpu.VMEM((tbm, d_c), BF16), pltpu.VMEM((halo + tbm, d_c), F32)],
        compiler_params=_params(("parallel",), _nbytes((tbm, four_dc), F32), _nbytes((halo, four_dc), F32),
                                _nbytes((tbm, 2 * d_c), BF16), _nbytes((halo + tbm, d_c), F32, 1),
                                _nbytes((SGU_CHUNK, d_c), F32)),
        name="mix_odd",
    )(p, p, row(ln_g), row(ln_b), sgu_w, sb, conv_w, row(conv_b), row(gn_g), row(gn_b))


_NO_ID = float(2 ** 20)


def _sorted_top16(x, sub):
    ng = PEER_NKEYS // V7X_SUBLANES
    xs = [x[v * V7X_SUBLANES:(v + 1) * V7X_SUBLANES, :] for v in range(ng)]
    ids = [sub + float(v * V7X_SUBLANES) for v in range(ng)]
    for rnd in range(ng):
        for v in range(rnd % 2, ng - 1, 2):
            a, b, ia, ib = xs[v], xs[v + 1], ids[v], ids[v + 1]
            swap = b > a
            xs[v], xs[v + 1] = jnp.maximum(a, b), jnp.minimum(a, b)
            ids[v], ids[v + 1] = jnp.where(swap, ib, ia), jnp.where(swap, ia, ib)
    vals, picks = [], []
    for n in range(PEER_TOPK):
        m = jnp.max(xs[0], axis=0, keepdims=True)
        sel = jnp.min(jnp.where(xs[0] == m, ids[0], _NO_ID), axis=0, keepdims=True)
        vals.append(m)
        picks.append(sel)
        win = ids[0] == sel
        for v in range(PEER_TOPK - 1 - n):
            xs[v] = jnp.where(win, xs[v + 1], xs[v])
            ids[v] = jnp.where(win, ids[v + 1], ids[v])
    return vals, picks


def _pair_top16(s0, i0, s1, i1, sub):
    half = PEER_TOPK // 2
    s0_lo, s0_hi = jnp.concatenate(s0[:half], axis=0), jnp.concatenate(s0[half:], axis=0)
    e0_lo = jnp.concatenate(i0[:half], axis=0) * float(PEER_NKEYS)
    e0_hi = jnp.concatenate(i0[half:], axis=0) * float(PEER_NKEYS)
    slot_v, slot_e = [], []
    for l in range(PEER_TOPK):
        v = s0_lo + s1[l]
        rows_valid = PEER_TOPK // (l + 1)
        if rows_valid < half:
            v = jnp.where(sub < float(rows_valid), v, -jnp.inf)
        slot_v.append(v)
        slot_e.append(e0_lo + i1[l])
    hi_v, hi_e = s0_hi + s1[0], e0_hi + i1[0]
    pos_lo = sub * float(PEER_TOPK)
    pos_hi = (sub + float(half)) * float(PEER_TOPK)
    sums, experts = [], []
    for n in range(PEER_TOPK):
        m = jnp.max(jnp.maximum(slot_v[0], hi_v), axis=0, keepdims=True)
        pick = jnp.min(jnp.minimum(jnp.where(slot_v[0] == m, pos_lo, _NO_ID),
                                   jnp.where(hi_v == m, pos_hi, _NO_ID)), axis=0, keepdims=True)
        win_lo, win_hi = pos_lo == pick, pos_hi == pick
        experts.append(jnp.max(jnp.maximum(jnp.where(win_lo, slot_e[0], -1.0),
                                           jnp.where(win_hi, hi_e, -1.0)), axis=0, keepdims=True))
        sums.append(m)
        pos_lo = jnp.where(win_lo, pos_lo + 1.0, pos_lo)
        hi_v = jnp.where(win_hi, -jnp.inf, hi_v)
        for l in range(PEER_TOPK - 1 - n):
            slot_v[l] = jnp.where(win_lo, slot_v[l + 1], slot_v[l])
            slot_e[l] = jnp.where(win_lo, slot_e[l + 1], slot_e[l])
    return sums, experts


def _select_kernel(h_ref, g_ref, wq_ref, keys_ref, xn_ref, e_ref, gate_ref, qt_ref, fs_ref, fe_ref, *, tbs):
    x = h_ref[...]
    ms = jnp.mean(x * x, axis=-1, keepdims=True)
    xn = (x * lax.rsqrt(ms + EPS) * g_ref[...]).astype(BF16)
    xn_ref[...] = xn
    qt_ref[...] = lax.dot_general(wq_ref[...], xn, (((1,), (1,)), ((), ())),
                                  preferred_element_type=F32).astype(BF16)

    sub =lax.broadcasted_iota(jnp.int32, (V7X_SUBLANES, tbs), 0).astype(F32)

    def head(hd, carry):
        tops = []
        for part in range(2):
            hp = hd * 2 + part
            q = qt_ref[pl.ds(pl.multiple_of(hp * PEER_DKH, PEER_DKH), PEER_DKH), :]
            scores = jnp.dot(keys_ref[hp], q, preferred_element_type=F32)
            tops.append(_sorted_top16(scores, sub))
        (s0, i0), (s1, i1) = tops
        f_s, f_e = _pair_top16(s0, i0, s1, i1, sub)
        f_s = jnp.concatenate(f_s, axis=0)
        ex = jnp.exp(f_s - f_s[0:1])
        gates = ex / jnp.sum(ex, axis=0, keepdims=True)
        rows = pl.ds(pl.multiple_of(hd * PEER_TOPK, PEER_TOPK), PEER_TOPK)
        fs_ref[rows, :] = gates
        fe_ref[rows, :] = jnp.concatenate(f_e, axis=0)
        return carry

    lax.fori_loop(0, PEER_HEADS, head, 0, unroll=4)
    gate_ref[...] = fs_ref[...].T
    e_ref[...] = fe_ref[...].T


def _select(h, g, wq_t, keys):
    t, d = h.shape
    tbs = _tile(t, 256)
    kern = functools.partial(_select_kernel, tbs=tbs)
    nq = wq_t.shape[0]
    return pl.pallas_call(
        kern,
        out_shape=(jax.ShapeDtypeStruct((t, d), BF16),
                   jax.ShapeDtypeStruct((t, N_SEL), F32),
                   jax.ShapeDtypeStruct((t, N_SEL), F32)),
        grid=(t // tbs,),
        in_specs=[pl.BlockSpec((tbs, d), lambda i: (i, 0)),
                  pl.BlockSpec((1, d), lambda i: (0, 0)),
                  pl.BlockSpec((nq, d), lambda i: (0, 0)),
                  pl.BlockSpec(keys.shape, lambda i: (0, 0, 0))],
        out_specs=(pl.BlockSpec((tbs, d), lambda i: (i, 0)),
                   pl.BlockSpec((tbs, N_SEL), lambda i: (i, 0)),
                   pl.BlockSpec((tbs, N_SEL), lambda i: (i, 0))),
        scratch_shapes=[pltpu.VMEM((nq, tbs), BF16),
                        pltpu.VMEM((N_SEL, tbs), F32),
                        pltpu.VMEM((N_SEL, tbs), F32)],
        compiler_params=_params(("parallel",), _nbytes((tbs, d), F32), _nbytes((nq, d), BF16),
                                _nbytes((tbs, d), BF16), _nbytes(keys.shape, BF16)),
        name="peer_select",
    )(h, g.reshape(1, d), wq_t, keys)


def _gates_kernel(e_ref, gate_ref, o_ref, tile_ref, *, tbg, pitch):
    sub_id = lax.broadcasted_iota(jnp.int32, (PEER_NKEYS, N_SEL), 0)

    def token(tk, carry):
        e = e_ref[pl.ds(tk, 1), :].astype(jnp.int32)
        gate = gate_ref[pl.ds(tk, 1), :]
        hot_i = jnp.where(sub_id == (e >> KEY_BITS), 1.0, 0.0).astype(BF16)
        hot_j = jnp.where(sub_id == (e & (PEER_NKEYS - 1)), gate, 0.0).astype(BF16)
        tile = lax.dot_general(hot_i, hot_j, (((1,), (1,)), ((), ())), preferred_element_type=F32)
        tile_ref[pl.ds(tk, PEER_NKEYS, stride=pitch), :] = tile
        return carry

    lax.fori_loop(0, tbg, token, 0, unroll=TOKEN_UNROLL)
    for i in range(PEER_NKEYS):
        o_ref[:, i * PEER_NKEYS:(i + 1) * PEER_NKEYS] = tile_ref[i * pitch:i * pitch + tbg, :].astype(BF16)


def _dense_gates(e, gate):
    t = e.shape[0]
    n_exp = PEER_NKEYS * PEER_NKEYS
    tbg = _tile(t, 128)
    pitch = tbg + V7X_SUBLANES
    kern = functools.partial(_gates_kernel, tbg=tbg, pitch=pitch)
    return pl.pallas_call(
        kern,
        out_shape=jax.ShapeDtypeStruct((t, n_exp), BF16),
        grid=(t // tbg,),
        in_specs=[pl.BlockSpec((tbg, N_SEL), lambda i: (i, 0)),
                  pl.BlockSpec((tbg, N_SEL), lambda i: (i, 0))],
        out_specs=pl.BlockSpec((tbg, n_exp), lambda i: (i, 0)),
        scratch_shapes=[pltpu.VMEM((PEER_NKEYS * pitch, PEER_NKEYS), F32)],
        compiler_params=_params(("parallel",), _nbytes((tbg, n_exp), BF16),
                                _nbytes((PEER_NKEYS * pitch, PEER_NKEYS), F32, 1)),
        name="peer_dense_gates",
    )(e, gate)


def _ffn_kernel(xn_ref, u_ref, v_ref, gt_ref, h_ref, fg_ref, o_ref, *, final_norm):
    n = pl.program_id(1)

    @pl.when(n == 0)
    def _():
        o_ref[...] = h_ref[...]

    s = lax.dot_general(xn_ref[...], u_ref[...], (((1,), (1,)), ((), ())), preferred_element_type=F32)
    act = (_gelu(s) * gt_ref[...].astype(F32)).astype(BF16)
    o_ref[...] += jnp.dot(act, v_ref[...], preferred_element_type=F32)

    if final_norm:
        @pl.when(n == pl.num_programs(1) - 1)
        def _():
            x = o_ref[...]
            ms = jnp.mean(x * x, axis=-1, keepdims=True)
            o_ref[...] = x * lax.rsqrt(ms + EPS) * fg_ref[...]


def _peer_ffn(xn, u, v, gt, h, final_g, *, final_norm):
    t, d = xn.shape
    n_exp = u.shape[0]
    tb, nb = _tile(t, 1024), _tile(n_exp, 512)
    kern = functools.partial(_ffn_kernel, final_norm=final_norm)
    return pl.pallas_call(
        kern,
        out_shape=jax.ShapeDtypeStruct((t, d), F32),
        grid=(t // tb, n_exp // nb),
        in_specs=[pl.BlockSpec((tb, d), lambda i, n: (i, 0)),
                  pl.BlockSpec((nb, d), lambda i, n: (n, 0)),
                  pl.BlockSpec((nb, d), lambda i, n: (n, 0)),
                  pl.BlockSpec((tb, nb), lambda i, n: (i, n)),
                  pl.BlockSpec((tb, d), lambda i, n: (i, 0), pipeline_mode=pl.Buffered(1)),
                  pl.BlockSpec((1, d), lambda i, n: (0, 0))],
        out_specs=pl.BlockSpec((tb, d), lambda i, n: (i, 0)),
        compiler_params=_params(("parallel", "arbitrary"), _nbytes((tb, d), BF16), 2 * _nbytes((nb, d), BF16),
                                _nbytes((tb, nb), BF16), _nbytes((tb, d), F32, 1), _nbytes((tb, d), F32),
                                _nbytes((tb, nb), F32, 1)),
        name="peer_ffn",
    )(xn, u, v, gt, h, final_g.reshape(1, d))


def kernel(x, mix_norm_g, ffn_norm_g, ev_w_in, ev_pool_w, ev_pool_scale, ev_conv_w, ev_w_out, od_w_in,
           od_v_norm_g, od_v_norm_b, od_sgu_w, od_sgu_b, od_conv_w, od_conv_b, od_gn_g, od_gn_b, od_w_out,
           peer_w_q, peer_subkeys, peer_u, peer_v, final_norm_g):
    bsz, seq, d = x.shape
    depth = mix_norm_g.shape[0]
    h = x.reshape(bsz * seq, d)
    for layer in range(depth):
        j = layer // 2
        if layer % 2 == 0:
            p = _norm_matmul(h, mix_norm_g[layer], ev_w_in[j].astype(BF16))
            y = _mix0(p, ev_pool_w[j].astype(BF16), ev_pool_scale[j], ev_conv_w[j], seq=seq)
            h = _matmul_residual(y, ev_w_out[j].astype(BF16), h)
        else:
            p = _norm_matmul(h, mix_norm_g[layer], od_w_in[j].astype(BF16))
            y = _mix1(p, od_v_norm_g[j], od_v_norm_b[j], od_sgu_w[j], od_sgu_b[j], od_conv_w[j],
                      od_conv_b[j], od_gn_g[j], od_gn_b[j], seq=seq)
            h = _matmul_residual(y, od_w_out[j].astype(BF16), h)
        keys = peer_subkeys[layer].reshape(2 * PEER_HEADS, PEER_NKEYS, PEER_DKH).astype(BF16)
        xn, e, gate = _select(h, ffn_norm_g[layer], peer_w_q[layer].T.astype(BF16), keys)
        gt = _dense_gates(e, gate)
        h = _peer_ffn(xn, peer_u[layer].astype(BF16), peer_v[layer].astype(BF16), gt, h, final_norm_g,
                      final_norm=(layer == depth - 1))
    return h.reshape(bsz, seq, d)
```

```python
import functools

import jax
import jax.numpy as jnp
from jax import lax
from jax.experimental import pallas as pl
from jax.experimental.pallas import tpu as pltpu

F32 = jnp.float32
BF16 = jnp.bfloat16
EPS = 1e-6

V7X_VMEM_BYTES = 64 * 1024 * 1024
V7X_LANES = 128
V7X_SUBLANES = 8
COMPILER_VMEM_ALLOWANCE = 12 * 1024 * 1024

POOL_WINDOWS = (2, 4, 8, 16)
SHORT_CONV = 3
SGU_CHUNK = 128
SGU_HEADS = 8
CONF_KERNEL = 31
CONF_GROUPS = 8
PEER_HEADS = 8
PEER_NKEYS = 128
KEY_BITS = 7
PEER_TOPK = 16
PEER_DKH = 64
N_SEL = PEER_HEADS * PEER_TOPK
MIX0_HALO = 16
MIX1_HALO = 32
TOKEN_UNROLL = 64


def _params(semantics, *buffer_bytes):
    limit = min(int(sum(buffer_bytes)) + COMPILER_VMEM_ALLOWANCE, V7X_VMEM_BYTES - 4 * 1024 * 1024)
    return pltpu.CompilerParams(dimension_semantics=semantics, vmem_limit_bytes=limit)


def _nbytes(shape, dtype, buffers=2):
    n = 1
    for s in shape:
        n *= s
    return n * jnp.dtype(dtype).itemsize * buffers


def _gelu(x):
    return 0.5 * x * (1.0 + lax.erf(x * (2.0 ** -0.5)))


def _tile(total, want):
    t = min(total, want)
    assert total % t == 0, (total, want)
    return t


def _cast_kernel(x_ref, o_ref):
    o_ref[...] = x_ref[...].astype(BF16)


def _layer_to_bf16(table, layer):
    _, rows, cols = table.shape
    tb = _tile(rows, 1024)
    return pl.pallas_call(
        _cast_kernel,
        out_shape=jax.ShapeDtypeStruct((rows, cols), BF16),
        grid=(rows // tb,),
        in_specs=[pl.BlockSpec((None, tb, cols), lambda i: (layer, i, 0))],
        out_specs=pl.BlockSpec((tb, cols), lambda i: (i, 0)),
        compiler_params=_params(("parallel",), _nbytes((tb, cols), F32), _nbytes((tb, cols), BF16)),
        name="table_to_bf16",
    )(table)


def _norm_mm_kernel(x_ref, g_ref, w_ref, o_ref, zn_ref):
    @pl.when(pl.program_id(1) == 0)
    def _():
        x = x_ref[...]
        ms = jnp.mean(x * x, axis=-1, keepdims=True)
        zn_ref[...] = (x * lax.rsqrt(ms + EPS) * g_ref[...]).astype(BF16)

    o_ref[...] = jnp.dot(zn_ref[...], w_ref[...], preferred_element_type=F32)


def _norm_matmul(x, g, w):
    t, d = x.shape
    n = w.shape[1]
    tb, tn = _tile(t, 1024), _tile(n, 1024)
    return pl.pallas_call(
        _norm_mm_kernel,
        out_shape=jax.ShapeDtypeStruct((t, n), F32),
        grid=(t // tb, n // tn),
        in_specs=[pl.BlockSpec((tb, d), lambda i, j: (i, 0)),
                  pl.BlockSpec((1, d), lambda i, j: (0, 0)),
                  pl.BlockSpec((d, tn), lambda i, j: (0, j))],
        out_specs=pl.BlockSpec((tb, tn), lambda i, j: (i, j)),
        scratch_shapes=[pltpu.VMEM((tb, d), BF16)],
        compiler_params=_params(("parallel", "arbitrary"), _nbytes((tb, d), F32), _nbytes((d, tn), BF16),
                                _nbytes((tb, tn), F32), _nbytes((tb, d), BF16, 1)),
        name="norm_matmul",
    )(x, g.reshape(1, d), w)


def _mm_res_kernel(y_ref, w_ref, h_ref, o_ref):
    o_ref[...] = h_ref[...] + jnp.dot(y_ref[...], w_ref[...], preferred_element_type=F32)


def _matmul_residual(y, w, h):
    t, k = y.shape
    n = w.shape[1]
    tb, tn = _tile(t, 1024), _tile(n, 1024)
    return pl.pallas_call(
        _mm_res_kernel,
        out_shape=jax.ShapeDtypeStruct((t, n), F32),
        grid=(t // tb, n // tn),
        in_specs=[pl.BlockSpec((tb, k), lambda i, j: (i, 0)),
                  pl.BlockSpec((k, tn), lambda i, j: (0, j)),
                  pl.BlockSpec((tb, tn), lambda i, j: (i, j))],
        out_specs=pl.BlockSpec((tb, tn), lambda i, j: (i, j)),
        compiler_params=_params(("parallel", "parallel"), _nbytes((tb, k), BF16), _nbytes((k, tn), BF16),
                                _nbytes((tb, tn), F32), _nbytes((tb, tn), F32)),
        name="matmul_residual",
    )(y, w, h)


def _mix0_kernel(p_ref, halo_ref, pw_ref, ps_ref, cw_ref, y_ref, ext_ref, *, tbm, seq, d_a):
    row0 = pl.program_id(0) * tbm
    at_seq_start = (row0 % seq) == 0
    halo = MIX0_HALO
    gc = d_a // len(POOL_WINDOWS)

    ext_ref[0:halo, 0:d_a] = jnp.where(at_seq_start, 0.0, halo_ref[:, 0:d_a])
    ext_ref[0:halo, d_a:2 * d_a] = jnp.where(
        at_seq_start, 0.0, halo_ref[:, 2 * d_a:3 * d_a] * halo_ref[:, 3 * d_a:4 * d_a])
    ext_ref[halo:halo + tbm, 0:d_a] = p_ref[:, 0:d_a]
    ext_ref[halo:halo + tbm, d_a:2 * d_a] = p_ref[:, 2 * d_a:3 * d_a] * p_ref[:, 3 * d_a:4 * d_a]

    pos = (row0 % seq) + lax.broadcasted_iota(jnp.int32, (tbm, gc), 0)
    for g, win in enumerate(POOL_WINDOWS):
        cs = slice(g * gc, (g + 1) * gc)
        x = ext_ref[halo:halo + tbm, cs]
        s = x
        for dlt in range(1, win):
            s = s + ext_ref[halo - dlt:halo - dlt + tbm, cs]
        cnt = jnp.minimum(pos + 1, win).astype(F32)
        pooled = s / cnt - x
        mixed = jnp.dot(pooled.astype(BF16), pw_ref[g], preferred_element_type=F32)
        y_ref[:, cs] = (mixed * ps_ref[:, cs]).astype(BF16)

    for c0 in range(0, d_a, 4 * V7X_LANES):
        cs = slice(c0, c0 + 4 * V7X_LANES)
        es = slice(d_a + c0, d_a + c0 + 4 * V7X_LANES)
        conv = cw_ref[SHORT_CONV - 1:SHORT_CONV, cs] * ext_ref[halo:halo + tbm, es]
        for dlt in range(1, SHORT_CONV):
            k = SHORT_CONV - 1 - dlt
            conv = conv + cw_ref[k:k + 1, cs] * ext_ref[halo - dlt:halo - dlt + tbm, es]
        bg = p_ref[:, d_a + c0:d_a + c0 + 4 * V7X_LANES]
        y_ref[:, d_a + c0:d_a + c0 + 4 * V7X_LANES] = (bg * conv).astype(BF16)


def _mix0(p, pool_w, pool_scale, conv_w, *, seq):
    t, four_da = p.shape
    d_a = four_da // 4
    tbm = _tile(seq, 256)
    halo = MIX0_HALO
    gc = d_a // len(POOL_WINDOWS)
    kern = functools.partial(_mix0_kernel, tbm=tbm, seq=seq, d_a=d_a)
    return pl.pallas_call(
        kern,
        out_shape=jax.ShapeDtypeStruct((t, 2 * d_a), BF16),
        grid=(t // tbm,),
        in_specs=[pl.BlockSpec((tbm, four_da), lambda i: (i, 0)),
                  pl.BlockSpec((halo, four_da), lambda i: (jnp.maximum(i * (tbm // halo) - 1, 0), 0)),
                  pl.BlockSpec((len(POOL_WINDOWS), gc, gc), lambda i: (0, 0, 0)),
                  pl.BlockSpec((1, d_a), lambda i: (0, 0)),
                  pl.BlockSpec((SHORT_CONV, d_a), lambda i: (0, 0))],
        out_specs=pl.BlockSpec((tbm, 2 * d_a), lambda i: (i, 0)),
        scratch_shapes=[pltpu.VMEM((halo + tbm, 2 * d_a), F32)],
        compiler_params=_params(("parallel",), _nbytes((tbm, four_da), F32), _nbytes((halo, four_da), F32),
                                _nbytes((tbm, 2 * d_a), BF16), _nbytes((halo + tbm, 2 * d_a), F32, 1)),
        name="mix_even",
    )(p, p, pool_w, pool_scale.reshape(1, d_a), conv_w)


def _mix1_kernel(p_ref, halo_ref, lng_ref, lnb_ref, ws_ref, sb_ref, cw_ref, cb_ref, gng_ref, gnb_ref,
                 y_ref, vn_ref, zext_ref, *, tbm, seq, d_c):
    row0 = pl.program_id(0) * tbm
    at_seq_start = (row0 % seq) == 0
    halo = MIX1_HALO
    hd = d_c // SGU_HEADS

    v = _gelu(p_ref[:, d_c:2 * d_c])
    mu = jnp.mean(v, axis=-1, keepdims=True)
    vc = v - mu
    var = jnp.mean(vc * vc, axis=-1, keepdims=True)
    vn_ref[...] = ((vc * lax.rsqrt(var + EPS)) * lng_ref[...] + lnb_ref[...]).astype(BF16)

    r = lax.broadcasted_iota(jnp.int32, (SGU_CHUNK, SGU_CHUNK), 0)
    c = lax.broadcasted_iota(jnp.int32, (SGU_CHUNK, SGU_CHUNK), 1)
    for h in range(SGU_HEADS):
        cs = slice(h * hd, (h + 1) * hd)
        wm = jnp.where(r >= c, ws_ref[h], 0.0).astype(BF16)
        for ch in range(tbm // SGU_CHUNK):
            rs = slice(ch * SGU_CHUNK, (ch + 1) * SGU_CHUNK)
            mixed = jnp.dot(wm, vn_ref[rs, cs], preferred_element_type=F32) + sb_ref[:, cs]
            u = _gelu(p_ref[rs, cs])
            y_ref[rs, cs] = (u * mixed).astype(BF16)

    zext_ref[0:halo, :] = jnp.where(
        at_seq_start, 0.0, halo_ref[:, 2 * d_c:3 * d_c] * jax.nn.sigmoid(halo_ref[:, 3 * d_c:4 * d_c]))
    zext_ref[halo:halo + tbm, :] = p_ref[:, 2 * d_c:3 * d_c] * jax.nn.sigmoid(p_ref[:, 3 * d_c:4 * d_c])
    gd = d_c // CONF_GROUPS
    first = halo - (CONF_KERNEL - 1)
    for gi in range(CONF_GROUPS):
        cs = slice(gi * gd, (gi + 1) * gd)
        acc = cb_ref[:, cs] + cw_ref[0:1, cs] * zext_ref[first:first + tbm, cs]
        for k in range(1, CONF_KERNEL):
            acc = acc + cw_ref[k:k + 1, cs] * zext_ref[first + k:first + k + tbm, cs]
        mu = jnp.mean(acc, axis=-1, keepdims=True)
        zc = acc - mu
        var = jnp.mean(zc * zc, axis=-1, keepdims=True)
        zn = (zc * lax.rsqrt(var + EPS)) * gng_ref[:, cs] + gnb_ref[:, cs]
        y_ref[:, d_c + gi * gd:d_c + (gi + 1) * gd] = (zn * jax.nn.sigmoid(zn)).astype(BF16)


def _mix1(p, ln_g, ln_b, sgu_w, sgu_b, conv_w, conv_b, gn_g, gn_b, *, seq):
    t, four_dc = p.shape
    d_c = four_dc // 4
    tbm = _tile(seq, 256)
    halo = MIX1_HALO
    hd = d_c // SGU_HEADS
    sb = jnp.repeat(sgu_b.T, hd, axis=1)
    row = lambda a: a.reshape(1, d_c)
    kern = functools.partial(_mix1_kernel, tbm=tbm, seq=seq, d_c=d_c)
    const2 = lambda i: (0, 0)
    return pl.pallas_call(
        kern,
        out_shape=jax.ShapeDtypeStruct((t, 2 * d_c), BF16),
        grid=(t // tbm,),
        in_specs=[pl.BlockSpec((tbm, four_dc), lambda i: (i, 0)),
                  pl.BlockSpec((halo, four_dc), lambda i: (jnp.maximum(i * (tbm // halo) - 1, 0), 0)),
                  pl.BlockSpec((1, d_c), const2),
                  pl.BlockSpec((1, d_c), const2),
                  pl.BlockSpec((SGU_HEADS, SGU_CHUNK, SGU_CHUNK), lambda i: (0, 0, 0)),
                  pl.BlockSpec((SGU_CHUNK, d_c), const2),
                  pl.BlockSpec((CONF_KERNEL, d_c), const2),
                  pl.BlockSpec((1, d_c), const2),
                  pl.BlockSpec((1, d_c), const2),
                  pl.BlockSpec((1, d_c), const2)],
        out_specs=pl.BlockSpec((tbm, 2 * d_c), lambda i: (i, 0)),
        scratch_shapes=[pltpu.VMEM((tbm, d_c), BF16), pltpu.VMEM((halo + tbm, d_c), F32)],
        compiler_params=_params(("parallel",), _nbytes((tbm, four_dc), F32), _nbytes((halo, four_dc), F32),
                                _nbytes((tbm, 2 * d_c), BF16), _nbytes((halo + tbm, d_c), F32, 1),
                                _nbytes((SGU_CHUNK, d_c), F32)),
        name="mix_odd",
    )(p, p, row(ln_g), row(ln_b), sgu_w, sb, conv_w, row(conv_b), row(gn_g), row(gn_b))


_NO_ID = float(2 ** 20)


def _sorted_top16(x, sub):
    ng = PEER_NKEYS // V7X_SUBLANES
    xs = [x[v * V7X_SUBLANES:(v + 1) * V7X_SUBLANES, :] for v in range(ng)]
    ids = [sub + float(v * V7X_SUBLANES) for v in range(ng)]
    for rnd in range(ng):
        for v in range(rnd % 2, ng - 1, 2):
            a, b, ia, ib = xs[v], xs[v + 1], ids[v], ids[v + 1]
            swap = b > a
            xs[v], xs[v + 1] = jnp.maximum(a, b), jnp.minimum(a, b)
            ids[v], ids[v + 1] = jnp.where(swap, ib, ia), jnp.where(swap, ia, ib)
    vals, picks = [], []
    for n in range(PEER_TOPK):
        m = jnp.max(xs[0], axis=0, keepdims=True)
        sel = jnp.min(jnp.where(xs[0] == m, ids[0], _NO_ID), axis=0, keepdims=True)
        vals.append(m)
        picks.append(sel)
        win = ids[0] == sel
        for v in range(PEER_TOPK - 1 - n):
            xs[v] = jnp.where(win, xs[v + 1], xs[v])
            ids[v] = jnp.where(win, ids[v + 1], ids[v])
    return vals, picks


def _pair_top16(s0, i0, s1, i1, sub):
    half = PEER_TOPK // 2
    s0_lo, s0_hi = jnp.concatenate(s0[:half], axis=0), jnp.concatenate(s0[half:], axis=0)
    e0_lo = jnp.concatenate(i0[:half], axis=0) * float(PEER_NKEYS)
    e0_hi = jnp.concatenate(i0[half:], axis=0) * float(PEER_NKEYS)
    slot_v, slot_e = [], []
    for l in range(PEER_TOPK):
        v = s0_lo + s1[l]
        rows_valid = PEER_TOPK // (l + 1)
        if rows_valid < half:
            v = jnp.where(sub < float(rows_valid), v, -jnp.inf)
        slot_v.append(v)
        slot_e.append(e0_lo + i1[l])
    hi_v, hi_e = s0_hi + s1[0], e0_hi + i1[0]
    pos_lo = sub * float(PEER_TOPK)
    pos_hi = (sub + float(half)) * float(PEER_TOPK)
    sums, experts = [], []
    for n in range(PEER_TOPK):
        m = jnp.max(jnp.maximum(slot_v[0], hi_v), axis=0, keepdims=True)
        pick = jnp.min(jnp.minimum(jnp.where(slot_v[0] == m, pos_lo, _NO_ID),
                                   jnp.where(hi_v == m, pos_hi, _NO_ID)), axis=0, keepdims=True)
        win_lo, win_hi = pos_lo == pick, pos_hi == pick
        experts.append(jnp.max(jnp.maximum(jnp.where(win_lo, slot_e[0], -1.0),
                                           jnp.where(win_hi, hi_e, -1.0)), axis=0, keepdims=True))
        sums.append(m)
        pos_lo = jnp.where(win_lo, pos_lo + 1.0, pos_lo)
        hi_v = jnp.where(win_hi, -jnp.inf, hi_v)
        for l in range(PEER_TOPK - 1 - n):
            slot_v[l] = jnp.where(win_lo, slot_v[l + 1], slot_v[l])
            slot_e[l] = jnp.where(win_lo, slot_e[l + 1], slot_e[l])
    return sums, experts


def _select_kernel(h_ref, g_ref, wq_ref, keys_ref, xn_ref, e_ref, gate_ref, qt_ref, fs_ref, fe_ref, *, tbs):
    x = h_ref[...]
    ms = jnp.mean(x * x, axis=-1, keepdims=True)
    xn = (x * lax.rsqrt(ms + EPS) * g_ref[...]).astype(BF16)
    xn_ref[...] = xn
    qt_ref[...] = lax.dot_general(wq_ref[...], xn, (((1,), (1,)), ((), ())),
                                  preferred_element_type=F32).astype(BF16)

    sub =lax.broadcasted_iota(jnp.int32, (V7X_SUBLANES, tbs), 0).astype(F32)

    def head(hd, carry):
        tops = []
        for part in range(2):
            hp = hd * 2 + part
            q = qt_ref[pl.ds(pl.multiple_of(hp * PEER_DKH, PEER_DKH), PEER_DKH), :]
            scores = jnp.dot(keys_ref[hp], q, preferred_element_type=F32)
            tops.append(_sorted_top16(scores, sub))
        (s0, i0), (s1, i1) = tops
        f_s, f_e = _pair_top16(s0, i0, s1, i1, sub)
        f_s = jnp.concatenate(f_s, axis=0)
        ex = jnp.exp(f_s - f_s[0:1])
        gates = ex / jnp.sum(ex, axis=0, keepdims=True)
        rows = pl.ds(pl.multiple_of(hd * PEER_TOPK, PEER_TOPK), PEER_TOPK)
        fs_ref[rows, :] = gates
        fe_ref[rows, :] = jnp.concatenate(f_e, axis=0)
        return carry

    lax.fori_loop(0, PEER_HEADS, head, 0, unroll=PEER_HEADS)
    gate_ref[...] = fs_ref[...].T
    e_ref[...] = fe_ref[...].T


def _select(h, g, wq_t, keys):
    t, d = h.shape
    tbs = _tile(t, 256)
    kern = functools.partial(_select_kernel, tbs=tbs)
    nq = wq_t.shape[0]
    return pl.pallas_call(
        kern,
        out_shape=(jax.ShapeDtypeStruct((t, d), BF16),
                   jax.ShapeDtypeStruct((t, N_SEL), F32),
                   jax.ShapeDtypeStruct((t, N_SEL), F32)),
        grid=(t // tbs,),
        in_specs=[pl.BlockSpec((tbs, d), lambda i: (i, 0)),
                  pl.BlockSpec((1, d), lambda i: (0, 0)),
                  pl.BlockSpec((nq, d), lambda i: (0, 0)),
                  pl.BlockSpec(keys.shape, lambda i: (0, 0, 0))],
        out_specs=(pl.BlockSpec((tbs, d), lambda i: (i, 0)),
                   pl.BlockSpec((tbs, N_SEL), lambda i: (i, 0)),
                   pl.BlockSpec((tbs, N_SEL), lambda i: (i, 0))),
        scratch_shapes=[pltpu.VMEM((nq, tbs), BF16),
                        pltpu.VMEM((N_SEL, tbs), F32),
                        pltpu.VMEM((N_SEL, tbs), F32)],
        compiler_params=_params(("parallel",), _nbytes((tbs, d), F32), _nbytes((nq, d), BF16),
                                _nbytes((tbs, d), BF16), _nbytes(keys.shape, BF16)),
        name="peer_select",
    )(h, g.reshape(1, d), wq_t, keys)


def _gates_kernel(e_ref, gate_ref, o_ref, tile_ref, *, tbg, pitch):
    sub_id = lax.broadcasted_iota(jnp.int32, (PEER_NKEYS, N_SEL), 0)

    def token(tk, carry):
        e = e_ref[pl.ds(tk, 1), :].astype(jnp.int32)
        gate = gate_ref[pl.ds(tk, 1), :]
        hot_i = jnp.where(sub_id == (e >> KEY_BITS), 1.0, 0.0).astype(BF16)
        hot_j = jnp.where(sub_id == (e & (PEER_NKEYS - 1)), gate, 0.0).astype(BF16)
        tile = lax.dot_general(hot_i, hot_j, (((1,), (1,)), ((), ())), preferred_element_type=F32)
        tile_ref[pl.ds(tk, PEER_NKEYS, stride=pitch), :] = tile
        return carry

    lax.fori_loop(0, tbg, token, 0, unroll=TOKEN_UNROLL)
    for i in range(PEER_NKEYS):
        o_ref[:, i * PEER_NKEYS:(i + 1) * PEER_NKEYS] = tile_ref[i * pitch:i * pitch + tbg, :].astype(BF16)


def _dense_gates(e, gate):
    t = e.shape[0]
    n_exp = PEER_NKEYS * PEER_NKEYS
    tbg = _tile(t, 128)
    pitch = tbg + V7X_SUBLANES
    kern = functools.partial(_gates_kernel, tbg=tbg, pitch=pitch)
    return pl.pallas_call(
        kern,
        out_shape=jax.ShapeDtypeStruct((t, n_exp), BF16),
        grid=(t // tbg,),
        in_specs=[pl.BlockSpec((tbg, N_SEL), lambda i: (i, 0)),
                  pl.BlockSpec((tbg, N_SEL), lambda i: (i, 0))],
        out_specs=pl.BlockSpec((tbg, n_exp), lambda i: (i, 0)),
        scratch_shapes=[pltpu.VMEM((PEER_NKEYS * pitch, PEER_NKEYS), F32)],
        compiler_params=_params(("parallel",), _nbytes((tbg, n_exp), BF16),
                                _nbytes((PEER_NKEYS * pitch, PEER_NKEYS), F32, 1)),
        name="peer_dense_gates",
    )(e, gate)


def _ffn_kernel(xn_ref, u_ref, v_ref, gt_ref, h_ref, fg_ref, o_ref, *, final_norm):
    n = pl.program_id(1)

    @pl.when(n == 0)
    def _():
        o_ref[...] = h_ref[...]

    s = lax.dot_general(xn_ref[...], u_ref[...], (((1,), (1,)), ((), ())), preferred_element_type=F32)
    act = (_gelu(s) * gt_ref[...].astype(F32)).astype(BF16)
    o_ref[...] += jnp.dot(act, v_ref[...], preferred_element_type=F32)

    if final_norm:
        @pl.when(n == pl.num_programs(1) - 1)
        def _():
            x = o_ref[...]
            ms = jnp.mean(x * x, axis=-1, keepdims=True)
            o_ref[...] = x * lax.rsqrt(ms + EPS) * fg_ref[...]


def _peer_ffn(xn, u, v, gt, h, final_g, *, final_norm):
    t, d = xn.shape
    n_exp = u.shape[0]
    tb, nb = _tile(t, 1024), _tile(n_exp, 1024)
    kern = functools.partial(_ffn_kernel, final_norm=final_norm)
    return pl.pallas_call(
        kern,
        out_shape=jax.ShapeDtypeStruct((t, d), F32),
        grid=(t // tb, n_exp // nb),
        in_specs=[pl.BlockSpec((tb, d), lambda i, n: (i, 0), pipeline_mode=pl.Buffered(1)),
                  pl.BlockSpec((nb, d), lambda i, n: (n, 0)),
                  pl.BlockSpec((nb, d), lambda i, n: (n, 0)),
                  pl.BlockSpec((tb, nb), lambda i, n: (i, n)),
                  pl.BlockSpec((tb, d), lambda i, n: (i, 0), pipeline_mode=pl.Buffered(1)),
                  pl.BlockSpec((1, d), lambda i, n: (0, 0))],
        out_specs=pl.BlockSpec((tb, d), lambda i, n: (i, 0)),
        compiler_params=_params(("parallel", "arbitrary"), _nbytes((tb, d), BF16, 1), 2 * _nbytes((nb, d), BF16),
                                _nbytes((tb, nb), BF16), _nbytes((tb, d), F32, 1), _nbytes((tb, d), F32),
                                _nbytes((tb, nb), F32, 1)),
        name="peer_ffn",
    )(xn, u, v, gt, h, final_g.reshape(1, d))


def kernel(x, mix_norm_g, ffn_norm_g, ev_w_in, ev_pool_w, ev_pool_scale, ev_conv_w, ev_w_out, od_w_in,
           od_v_norm_g, od_v_norm_b, od_sgu_w, od_sgu_b, od_conv_w, od_conv_b, od_gn_g, od_gn_b, od_w_out,
           peer_w_q, peer_subkeys, peer_u, peer_v, final_norm_g):
    bsz, seq, d = x.shape
    depth = mix_norm_g.shape[0]
    h = x.reshape(bsz * seq, d)
    for layer in range(depth):
        j = layer // 2
        if layer % 2 == 0:
            p = _norm_matmul(h, mix_norm_g[layer], ev_w_in[j].astype(BF16))
            y = _mix0(p, ev_pool_w[j].astype(BF16), ev_pool_scale[j], ev_conv_w[j], seq=seq)
            h = _matmul_residual(y, ev_w_out[j].astype(BF16), h)
        else:
            p = _norm_matmul(h, mix_norm_g[layer], od_w_in[j].astype(BF16))
            y = _mix1(p, od_v_norm_g[j], od_v_norm_b[j], od_sgu_w[j], od_sgu_b[j], od_conv_w[j],
                      od_conv_b[j], od_gn_g[j], od_gn_b[j], seq=seq)
            h = _matmul_residual(y, od_w_out[j].astype(BF16), h)
        keys = peer_subkeys[layer].reshape(2 * PEER_HEADS, PEER_NKEYS, PEER_DKH).astype(BF16)
        xn, e, gate = _select(h, ffn_norm_g[layer], peer_w_q[layer].T.astype(BF16), keys)
        gt = _dense_gates(e, gate)
        h = _peer_ffn(xn, _layer_to_bf16(peer_u, layer), _layer_to_bf16(peer_v, layer), gt, h, final_norm_g,
                      final_norm=(layer == depth - 1))
    return h.reshape(bsz, seq, d)
```

```python
import functools

import jax
import jax.numpy as jnp
from jax import lax
from jax.experimental import pallas as pl
from jax.experimental.pallas import tpu as pltpu

F32 = jnp.float32
BF16 = jnp.bfloat16
EPS = 1e-6

V7X_VMEM_BYTES = 64 * 1024 * 1024
V7X_LANES = 128
V7X_SUBLANES = 8
COMPILER_VMEM_ALLOWANCE = 12 * 1024 * 1024

POOL_WINDOWS = (2, 4, 8, 16)
SHORT_CONV = 3
SGU_CHUNK = 128
SGU_HEADS = 8
CONF_KERNEL = 31
CONF_GROUPS = 8
PEER_HEADS = 8
PEER_NKEYS = 128
KEY_BITS = 7
PEER_TOPK = 16
PEER_DKH = 64
N_SEL = PEER_HEADS * PEER_TOPK
MIX0_HALO = 16
MIX1_HALO = 32
TOKEN_UNROLL = 64


def _params(semantics, *buffer_bytes):
    limit = min(int(sum(buffer_bytes)) + COMPILER_VMEM_ALLOWANCE, V7X_VMEM_BYTES - 4 * 1024 * 1024)
    return pltpu.CompilerParams(dimension_semantics=semantics, vmem_limit_bytes=limit)


def _nbytes(shape, dtype, buffers=2):
    n = 1
    for s in shape:
        n *= s
    return n * jnp.dtype(dtype).itemsize * buffers


def _gelu(x):
    return 0.5 * x * (1.0 + lax.erf(x * (2.0 ** -0.5)))


def _tile(total, want):
    t = min(total, want)
    assert total % t == 0, (total, want)
    return t


def _cast_kernel(x_ref, o_ref):
    o_ref[...] = x_ref[...].astype(BF16)


def _layer_to_bf16(table, layer):
    _, rows, cols = table.shape
    tb = _tile(rows, 1024)
    return pl.pallas_call(
        _cast_kernel,
        out_shape=jax.ShapeDtypeStruct((rows, cols), BF16),
        grid=(rows // tb,),
        in_specs=[pl.BlockSpec((None, tb, cols), lambda i: (layer, i, 0))],
        out_specs=pl.BlockSpec((tb, cols), lambda i: (i, 0)),
        compiler_params=_params(("parallel",), _nbytes((tb, cols), F32), _nbytes((tb, cols), BF16)),
        name="table_to_bf16",
    )(table)


def _norm_mm_kernel(x_ref, g_ref, w_ref, o_ref, zn_ref):
    @pl.when(pl.program_id(1) == 0)
    def _():
        x = x_ref[...]
        ms = jnp.mean(x * x, axis=-1, keepdims=True)
        zn_ref[...] = (x * lax.rsqrt(ms + EPS) * g_ref[...]).astype(BF16)

    o_ref[...] = jnp.dot(zn_ref[...], w_ref[...], preferred_element_type=F32)


def _norm_matmul(x, g, w):
    t, d = x.shape
    n = w.shape[1]
    tb, tn = _tile(t, 1024), _tile(n, 2048)
    return pl.pallas_call(
        _norm_mm_kernel,
        out_shape=jax.ShapeDtypeStruct((t, n), F32),
        grid=(t // tb, n // tn),
        in_specs=[pl.BlockSpec((tb, d), lambda i, j: (i, 0)),
                  pl.BlockSpec((1, d), lambda i, j: (0, 0)),
                  pl.BlockSpec((d, tn), lambda i, j: (0, j))],
        out_specs=pl.BlockSpec((tb, tn), lambda i, j: (i, j)),
        scratch_shapes=[pltpu.VMEM((tb, d), BF16)],
        compiler_params=_params(("parallel", "arbitrary"), _nbytes((tb, d), F32), _nbytes((d, tn), BF16),
                                _nbytes((tb, tn), F32), _nbytes((tb, d), BF16, 1)),
        name="norm_matmul",
    )(x, g.reshape(1, d), w)


def _mm_res_kernel(y_ref, w_ref, h_ref, o_ref):
    o_ref[...] = h_ref[...] + jnp.dot(y_ref[...], w_ref[...], preferred_element_type=F32)


def _matmul_residual(y, w, h):
    t, k = y.shape
    n = w.shape[1]
    tb, tn = _tile(t, 1024), _tile(n, 1024)
    return pl.pallas_call(
        _mm_res_kernel,
        out_shape=jax.ShapeDtypeStruct((t, n), F32),
        grid=(t // tb, n // tn),
        in_specs=[pl.BlockSpec((tb, k), lambda i, j: (i, 0)),
                  pl.BlockSpec((k, tn), lambda i, j: (0, j)),
                  pl.BlockSpec((tb, tn), lambda i, j: (i, j))],
        out_specs=pl.BlockSpec((tb, tn), lambda i, j: (i, j)),
        compiler_params=_params(("parallel", "parallel"), _nbytes((tb, k), BF16), _nbytes((k, tn), BF16),
                                _nbytes((tb, tn), F32), _nbytes((tb, tn), F32)),
        name="matmul_residual",
    )(y, w, h)


def _mix0_kernel(p_ref, halo_ref, pw_ref, ps_ref, cw_ref, y_ref, ext_ref, *, tbm, seq, d_a):
    row0 = pl.program_id(0) * tbm
    at_seq_start = (row0 % seq) == 0
    halo = MIX0_HALO
    gc = d_a // len(POOL_WINDOWS)

    ext_ref[0:halo, 0:d_a] = jnp.where(at_seq_start, 0.0, halo_ref[:, 0:d_a])
    ext_ref[0:halo, d_a:2 * d_a] = jnp.where(
        at_seq_start, 0.0, halo_ref[:, 2 * d_a:3 * d_a] * halo_ref[:, 3 * d_a:4 * d_a])
    ext_ref[halo:halo + tbm, 0:d_a] = p_ref[:, 0:d_a]
    ext_ref[halo:halo + tbm, d_a:2 * d_a] = p_ref[:, 2 * d_a:3 * d_a] * p_ref[:, 3 * d_a:4 * d_a]

    pos = (row0 % seq) + lax.broadcasted_iota(jnp.int32, (tbm, gc), 0)
    for g, win in enumerate(POOL_WINDOWS):
        cs = slice(g * gc, (g + 1) * gc)
        x = ext_ref[halo:halo + tbm, cs]
        s = x
        for dlt in range(1, win):
            s = s + ext_ref[halo - dlt:halo - dlt + tbm, cs]
        cnt = jnp.minimum(pos + 1, win).astype(F32)
        pooled = s / cnt - x
        mixed = jnp.dot(pooled.astype(BF16), pw_ref[g], preferred_element_type=F32)
        y_ref[:, cs] = (mixed * ps_ref[:, cs]).astype(BF16)

    for c0 in range(0, d_a, 4 * V7X_LANES):
        cs = slice(c0, c0 + 4 * V7X_LANES)
        es = slice(d_a + c0, d_a + c0 + 4 * V7X_LANES)
        conv = cw_ref[SHORT_CONV - 1:SHORT_CONV, cs] * ext_ref[halo:halo + tbm, es]
        for dlt in range(1, SHORT_CONV):
            k = SHORT_CONV - 1 - dlt
            conv = conv + cw_ref[k:k + 1, cs] * ext_ref[halo - dlt:halo - dlt + tbm, es]
        bg = p_ref[:, d_a + c0:d_a + c0 + 4 * V7X_LANES]
        y_ref[:, d_a + c0:d_a + c0 + 4 * V7X_LANES] = (bg * conv).astype(BF16)


def _mix0(p, pool_w, pool_scale, conv_w, *, seq):
    t, four_da = p.shape
    d_a = four_da // 4
    tbm = _tile(seq, 256)
    halo = MIX0_HALO
    gc = d_a // len(POOL_WINDOWS)
    kern = functools.partial(_mix0_kernel, tbm=tbm, seq=seq, d_a=d_a)
    return pl.pallas_call(
        kern,
        out_shape=jax.ShapeDtypeStruct((t, 2 * d_a), BF16),
        grid=(t // tbm,),
        in_specs=[pl.BlockSpec((tbm, four_da), lambda i: (i, 0)),
                  pl.BlockSpec((halo, four_da), lambda i: (jnp.maximum(i * (tbm // halo) - 1, 0), 0)),
                  pl.BlockSpec((len(POOL_WINDOWS), gc, gc), lambda i: (0, 0, 0)),
                  pl.BlockSpec((1, d_a), lambda i: (0, 0)),
                  pl.BlockSpec((SHORT_CONV, d_a), lambda i: (0, 0))],
        out_specs=pl.BlockSpec((tbm, 2 * d_a), lambda i: (i, 0)),
        scratch_shapes=[pltpu.VMEM((halo + tbm, 2 * d_a), F32)],
        compiler_params=_params(("parallel",), _nbytes((tbm, four_da), F32), _nbytes((halo, four_da), F32),
                                _nbytes((tbm, 2 * d_a), BF16), _nbytes((halo + tbm, 2 * d_a), F32, 1)),
        name="mix_even",
    )(p, p, pool_w, pool_scale.reshape(1, d_a), conv_w)


def _mix1_kernel(p_ref, halo_ref, lng_ref, lnb_ref, ws_ref, sb_ref, cw_ref, cb_ref, gng_ref, gnb_ref,
                 y_ref, vn_ref, zext_ref, shift_ref, *, tbm, seq, d_c):
    row0 = pl.program_id(0) * tbm
    at_seq_start = (row0 % seq) == 0
    halo = MIX1_HALO
    hd = d_c // SGU_HEADS

    v = _gelu(p_ref[:, d_c:2 * d_c])
    mu = jnp.mean(v, axis=-1, keepdims=True)
    vc = v - mu
    var = jnp.mean(vc * vc, axis=-1, keepdims=True)
    vn_ref[...] = ((vc * lax.rsqrt(var + EPS)) * lng_ref[...] + lnb_ref[...]).astype(BF16)

    r = lax.broadcasted_iota(jnp.int32, (SGU_CHUNK, SGU_CHUNK), 0)
    c = lax.broadcasted_iota(jnp.int32, (SGU_CHUNK, SGU_CHUNK), 1)
    for h in range(SGU_HEADS):
        cs = slice(h * hd, (h + 1) * hd)
        wm = jnp.where(r >= c, ws_ref[h], 0.0).astype(BF16)
        for ch in range(tbm // SGU_CHUNK):
            rs = slice(ch * SGU_CHUNK, (ch + 1) * SGU_CHUNK)
            mixed = jnp.dot(wm, vn_ref[rs, cs], preferred_element_type=F32) + sb_ref[:, cs]
            u = _gelu(p_ref[rs, cs])
            y_ref[rs, cs] = (u * mixed).astype(BF16)

    zext_ref[0:halo, :] = jnp.where(
        at_seq_start, 0.0, halo_ref[:, 2 * d_c:3 * d_c] * jax.nn.sigmoid(halo_ref[:, 3 * d_c:4 * d_c]))
    zext_ref[halo:halo + tbm, :] = p_ref[:, 2 * d_c:3 * d_c] * jax.nn.sigmoid(p_ref[:, 3 * d_c:4 * d_c])
    gd = d_c // CONF_GROUPS
    first = halo - (CONF_KERNEL - 1)
    for gi in range(CONF_GROUPS):
        cs = slice(gi * gd, (gi + 1) * gd)
        acc = jnp.broadcast_to(cb_ref[:, cs], (tbm, gd))
        for r in range(V7X_SUBLANES):
            taps = range(r, CONF_KERNEL, V7X_SUBLANES)
            rows = taps[-1] - r + tbm
            shift_ref[0:rows, :] = zext_ref[first + r:first + r + rows, cs]
            for k in taps:
                acc = acc + cw_ref[k:k + 1, cs] * shift_ref[k - r:k - r + tbm, :]
        mu = jnp.mean(acc, axis=-1, keepdims=True)
        zc = acc - mu
        var = jnp.mean(zc * zc, axis=-1, keepdims=True)
        zn = (zc * lax.rsqrt(var + EPS)) * gng_ref[:, cs] + gnb_ref[:, cs]
        y_ref[:, d_c + gi * gd:d_c + (gi + 1) * gd] = (zn * jax.nn.sigmoid(zn)).astype(BF16)


def _mix1(p, ln_g, ln_b, sgu_w, sgu_b, conv_w, conv_b, gn_g, gn_b, *, seq):
    t, four_dc = p.shape
    d_c = four_dc // 4
    tbm = _tile(seq, 256)
    halo = MIX1_HALO
    hd = d_c // SGU_HEADS
    sb = jnp.repeat(sgu_b.T, hd, axis=1)
    row = lambda a: a.reshape(1, d_c)
    kern = functools.partial(_mix1_kernel, tbm=tbm, seq=seq, d_c=d_c)
    const2 = lambda i: (0, 0)
    return pl.pallas_call(
        kern,
        out_shape=jax.ShapeDtypeStruct((t, 2 * d_c), BF16),
        grid=(t // tbm,),
        in_specs=[pl.BlockSpec((tbm, four_dc), lambda i: (i, 0)),
                  pl.BlockSpec((halo, four_dc), lambda i: (jnp.maximum(i * (tbm // halo) - 1, 0), 0)),
                  pl.BlockSpec((1, d_c), const2),
                  pl.BlockSpec((1, d_c), const2),
                  pl.BlockSpec((SGU_HEADS, SGU_CHUNK, SGU_CHUNK), lambda i: (0, 0, 0)),
                  pl.BlockSpec((SGU_CHUNK, d_c), const2),
                  pl.BlockSpec((CONF_KERNEL, d_c), const2),
                  pl.BlockSpec((1, d_c), const2),
                  pl.BlockSpec((1, d_c), const2),
                  pl.BlockSpec((1, d_c), const2)],
        out_specs=pl.BlockSpec((tbm, 2 * d_c), lambda i: (i, 0)),
        scratch_shapes=[pltpu.VMEM((tbm, d_c), BF16), pltpu.VMEM((halo + tbm, d_c), F32),
                        pltpu.VMEM((halo + tbm, d_c // CONF_GROUPS), F32)],
        compiler_params=_params(("parallel",), _nbytes((tbm, four_dc), F32), _nbytes((halo, four_dc), F32),
                                _nbytes((tbm, 2 * d_c), BF16), _nbytes((halo + tbm, d_c), F32, 1),
                                _nbytes((SGU_CHUNK, d_c), F32)),
        name="mix_odd",
    )(p, p, row(ln_g), row(ln_b), sgu_w, sb, conv_w, row(conv_b), row(gn_g), row(gn_b))


_NO_ID = float(2 ** 20)


def _sorted_top16(x, sub):
    ng = PEER_NKEYS // V7X_SUBLANES
    xs = [x[v * V7X_SUBLANES:(v + 1) * V7X_SUBLANES, :] for v in range(ng)]
    ids = [sub + float(v * V7X_SUBLANES) for v in range(ng)]
    for rnd in range(ng):
        for v in range(rnd % 2, ng - 1, 2):
            a, b, ia, ib = xs[v], xs[v + 1], ids[v], ids[v + 1]
            swap = b > a
            xs[v], xs[v + 1] = jnp.maximum(a, b), jnp.minimum(a, b)
            ids[v], ids[v + 1] = jnp.where(swap, ib, ia), jnp.where(swap, ia, ib)
    vals, picks = [], []
    for n in range(PEER_TOPK):
        m = jnp.max(xs[0], axis=0, keepdims=True)
        sel = jnp.min(jnp.where(xs[0] == m, ids[0], _NO_ID), axis=0, keepdims=True)
        vals.append(m)
        picks.append(sel)
        win = ids[0] == sel
        for v in range(PEER_TOPK - 1 - n):
            xs[v] = jnp.where(win, xs[v + 1], xs[v])
            ids[v] = jnp.where(win, ids[v + 1], ids[v])
    return vals, picks


def _pair_top16(s0, i0, s1, i1, sub):
    half = PEER_TOPK // 2
    s0_lo, s0_hi = jnp.concatenate(s0[:half], axis=0), jnp.concatenate(s0[half:], axis=0)
    e0_lo = jnp.concatenate(i0[:half], axis=0) * float(PEER_NKEYS)
    e0_hi = jnp.concatenate(i0[half:], axis=0) * float(PEER_NKEYS)
    slot_v, slot_e = [], []
    for l in range(PEER_TOPK):
        v = s0_lo + s1[l]
        rows_valid = PEER_TOPK // (l + 1)
        if rows_valid < half:
            v = jnp.where(sub < float(rows_valid), v, -jnp.inf)
        slot_v.append(v)
        slot_e.append(e0_lo + i1[l])
    hi_v, hi_e = s0_hi + s1[0], e0_hi + i1[0]
    pos_lo = sub * float(PEER_TOPK)
    pos_hi = (sub + float(half)) * float(PEER_TOPK)
    sums, experts = [], []
    for n in range(PEER_TOPK):
        m = jnp.max(jnp.maximum(slot_v[0], hi_v), axis=0, keepdims=True)
        pick = jnp.min(jnp.minimum(jnp.where(slot_v[0] == m, pos_lo, _NO_ID),
                                   jnp.where(hi_v == m, pos_hi, _NO_ID)), axis=0, keepdims=True)
        win_lo, win_hi = pos_lo == pick, pos_hi == pick
        experts.append(jnp.max(jnp.maximum(jnp.where(win_lo, slot_e[0], -1.0),
                                           jnp.where(win_hi, hi_e, -1.0)), axis=0, keepdims=True))
        sums.append(m)
        pos_lo = jnp.where(win_lo, pos_lo + 1.0, pos_lo)
        hi_v = jnp.where(win_hi, -jnp.inf, hi_v)
        for l in range(PEER_TOPK - 1 - n):
            slot_v[l] = jnp.where(win_lo, slot_v[l + 1], slot_v[l])
            slot_e[l] = jnp.where(win_lo, slot_e[l + 1], slot_e[l])
    return sums, experts


def _select_kernel(h_ref, g_ref, wq_ref, keys_ref, xn_ref, e_ref, gate_ref, qt_ref, fs_ref, fe_ref, *, tbs):
    x = h_ref[...]
    ms = jnp.mean(x * x, axis=-1, keepdims=True)
    xn = (x * lax.rsqrt(ms + EPS) * g_ref[...]).astype(BF16)
    xn_ref[...] = xn
    qt_ref[...] = lax.dot_general(wq_ref[...], xn, (((1,), (1,)), ((), ())),
                                  preferred_element_type=F32).astype(BF16)

    sub =lax.broadcasted_iota(jnp.int32, (V7X_SUBLANES, tbs), 0).astype(F32)

    def head(hd, carry):
        tops = []
        for part in range(2):
            hp = hd * 2 + part
            q = qt_ref[pl.ds(pl.multiple_of(hp * PEER_DKH, PEER_DKH), PEER_DKH), :]
            scores = jnp.dot(keys_ref[hp], q, preferred_element_type=F32)
            tops.append(_sorted_top16(scores, sub))
        (s0, i0), (s1, i1) = tops
        f_s, f_e = _pair_top16(s0, i0, s1, i1, sub)
        f_s = jnp.concatenate(f_s, axis=0)
        ex = jnp.exp(f_s - f_s[0:1])
        gates = ex / jnp.sum(ex, axis=0, keepdims=True)
        rows = pl.ds(pl.multiple_of(hd * PEER_TOPK, PEER_TOPK), PEER_TOPK)
        fs_ref[rows, :] = gates
        fe_ref[rows, :] = jnp.concatenate(f_e, axis=0)
        return carry

    lax.fori_loop(0, PEER_HEADS, head, 0, unroll=PEER_HEADS)
    gate_ref[...] = fs_ref[...].T
    e_ref[...] = fe_ref[...].T


def _select(h, g, wq_t, keys):
    t, d = h.shape
    tbs = _tile(t, 256)
    kern = functools.partial(_select_kernel, tbs=tbs)
    nq = wq_t.shape[0]
    return pl.pallas_call(
        kern,
        out_shape=(jax.ShapeDtypeStruct((t, d), BF16),
                   jax.ShapeDtypeStruct((t, N_SEL), F32),
                   jax.ShapeDtypeStruct((t, N_SEL), F32)),
        grid=(t // tbs,),
        in_specs=[pl.BlockSpec((tbs, d), lambda i: (i, 0)),
                  pl.BlockSpec((1, d), lambda i: (0, 0)),
                  pl.BlockSpec((nq, d), lambda i: (0, 0)),
                  pl.BlockSpec(keys.shape, lambda i: (0, 0, 0))],
        out_specs=(pl.BlockSpec((tbs, d), lambda i: (i, 0)),
                   pl.BlockSpec((tbs, N_SEL), lambda i: (i, 0)),
                   pl.BlockSpec((tbs, N_SEL), lambda i: (i, 0))),
        scratch_shapes=[pltpu.VMEM((nq, tbs), BF16),
                        pltpu.VMEM((N_SEL, tbs), F32),
                        pltpu.VMEM((N_SEL, tbs), F32)],
        compiler_params=_params(("parallel",), _nbytes((tbs, d), F32), _nbytes((nq, d), BF16),
                                _nbytes((tbs, d), BF16), _nbytes(keys.shape, BF16)),
        name="peer_select",
    )(h, g.reshape(1, d), wq_t, keys)


def _gates_kernel(e_ref, gate_ref, o_ref, tile_ref, *, tbg, pitch):
    sub_id = lax.broadcasted_iota(jnp.int32, (PEER_NKEYS, N_SEL), 0)

    def token(tk, carry):
        e = e_ref[pl.ds(tk, 1), :].astype(jnp.int32)
        gate = gate_ref[pl.ds(tk, 1), :]
        hot_i = jnp.where(sub_id == (e >> KEY_BITS), 1.0, 0.0).astype(BF16)
        hot_j = jnp.where(sub_id == (e & (PEER_NKEYS - 1)), gate, 0.0).astype(BF16)
        tile = lax.dot_general(hot_i, hot_j, (((1,), (1,)), ((), ())), preferred_element_type=F32)
        tile_ref[pl.ds(tk, PEER_NKEYS, stride=pitch), :] = tile
        return carry

    lax.fori_loop(0, tbg, token, 0, unroll=TOKEN_UNROLL)
    for i in range(PEER_NKEYS):
        o_ref[:, i * PEER_NKEYS:(i + 1) * PEER_NKEYS] = tile_ref[i * pitch:i * pitch + tbg, :].astype(BF16)


def _dense_gates(e, gate):
    t = e.shape[0]
    n_exp = PEER_NKEYS * PEER_NKEYS
    tbg = _tile(t, 256)
    pitch = tbg + V7X_SUBLANES
    kern = functools.partial(_gates_kernel, tbg=tbg, pitch=pitch)
    return pl.pallas_call(
        kern,
        out_shape=jax.ShapeDtypeStruct((t, n_exp), BF16),
        grid=(t // tbg,),
        in_specs=[pl.BlockSpec((tbg, N_SEL), lambda i: (i, 0)),
                  pl.BlockSpec((tbg, N_SEL), lambda i: (i, 0))],
        out_specs=pl.BlockSpec((tbg, n_exp), lambda i: (i, 0)),
        scratch_shapes=[pltpu.VMEM((PEER_NKEYS * pitch, PEER_NKEYS), F32)],
        compiler_params=_params(("parallel",), _nbytes((tbg, n_exp), BF16),
                                _nbytes((PEER_NKEYS * pitch, PEER_NKEYS), F32, 1)),
        name="peer_dense_gates",
    )(e, gate)


def _ffn_kernel(xn_ref, u_ref, v_ref, gt_ref, h_ref, fg_ref, o_ref, *, final_norm):
    n = pl.program_id(1)

    @pl.when(n == 0)
    def _():
        o_ref[...] = h_ref[...]

    s = lax.dot_general(xn_ref[...], u_ref[...], (((1,), (1,)), ((), ())), preferred_element_type=F32)
    act = (_gelu(s) * gt_ref[...].astype(F32)).astype(BF16)
    o_ref[...] += jnp.dot(act, v_ref[...], preferred_element_type=F32)

    if final_norm:
        @pl.when(n == pl.num_programs(1) - 1)
        def _():
            x = o_ref[...]
            ms = jnp.mean(x * x, axis=-1, keepdims=True)
            o_ref[...] = x * lax.rsqrt(ms + EPS) * fg_ref[...]


def _peer_ffn(xn, u, v, gt, h, final_g, *, final_norm):
    t, d = xn.shape
    n_exp = u.shape[0]
    tb, nb = _tile(t, 1024), _tile(n_exp, 1024)
    kern = functools.partial(_ffn_kernel, final_norm=final_norm)
    return pl.pallas_call(
        kern,
        out_shape=jax.ShapeDtypeStruct((t, d), F32),
        grid=(t // tb, n_exp // nb),
        in_specs=[pl.BlockSpec((tb, d), lambda i, n: (i, 0), pipeline_mode=pl.Buffered(1)),
                  pl.BlockSpec((nb, d), lambda i, n: (n, 0)),
                  pl.BlockSpec((nb, d), lambda i, n: (n, 0)),
                  pl.BlockSpec((tb, nb), lambda i, n: (i, n)),
                  pl.BlockSpec((tb, d), lambda i, n: (i, 0), pipeline_mode=pl.Buffered(1)),
                  pl.BlockSpec((1, d), lambda i, n: (0, 0))],
        out_specs=pl.BlockSpec((tb, d), lambda i, n: (i, 0)),
        compiler_params=_params(("parallel", "arbitrary"), _nbytes((tb, d), BF16, 1), 2 * _nbytes((nb, d), BF16),
                                _nbytes((tb, nb), BF16), _nbytes((tb, d), F32, 1), _nbytes((tb, d), F32),
                                _nbytes((tb, nb), F32, 1)),
        name="peer_ffn",
    )(xn, u, v, gt, h, final_g.reshape(1, d))


def kernel(x, mix_norm_g, ffn_norm_g, ev_w_in, ev_pool_w, ev_pool_scale, ev_conv_w, ev_w_out, od_w_in,
           od_v_norm_g, od_v_norm_b, od_sgu_w, od_sgu_b, od_conv_w, od_conv_b, od_gn_g, od_gn_b, od_w_out,
           peer_w_q, peer_subkeys, peer_u, peer_v, final_norm_g):
    bsz, seq, d = x.shape
    depth = mix_norm_g.shape[0]
    h = x.reshape(bsz * seq, d)
    for layer in range(depth):
        j = layer // 2
        if layer % 2 == 0:
            p = _norm_matmul(h, mix_norm_g[layer], ev_w_in[j].astype(BF16))
            y = _mix0(p, ev_pool_w[j].astype(BF16), ev_pool_scale[j], ev_conv_w[j], seq=seq)
            h = _matmul_residual(y, ev_w_out[j].astype(BF16), h)
        else:
            p = _norm_matmul(h, mix_norm_g[layer], od_w_in[j].astype(BF16))
            y = _mix1(p, od_v_norm_g[j], od_v_norm_b[j], od_sgu_w[j], od_sgu_b[j], od_conv_w[j],
                      od_conv_b[j], od_gn_g[j], od_gn_b[j], seq=seq)
            h = _matmul_residual(y, od_w_out[j].astype(BF16), h)
        keys = peer_subkeys[layer].reshape(2 * PEER_HEADS, PEER_NKEYS, PEER_DKH).astype(BF16)
        xn, e, gate = _select(h, ffn_norm_g[layer], peer_w_q[layer].T.astype(BF16), keys)
        gt = _dense_gates(e, gate)
        h = _peer_ffn(xn, _layer_to_bf16(peer_u, layer), _layer_to_bf16(peer_v, layer), gt, h, final_norm_g,
                      final_norm=(layer == depth - 1))
    return h.reshape(bsz, seq, d)
```

```python
import functools

import jax
import jax.numpy as jnp
from jax import lax
from jax.experimental import pallas as pl
from jax.experimental.pallas import tpu as pltpu

F32 = jnp.float32
BF16 = jnp.bfloat16
EPS = 1e-6

V7X_VMEM_BYTES = 64 * 1024 * 1024
V7X_LANES = 128
V7X_SUBLANES = 8
COMPILER_VMEM_ALLOWANCE = 12 * 1024 * 1024

POOL_WINDOWS = (2, 4, 8, 16)
SHORT_CONV = 3
SGU_CHUNK = 128
SGU_HEADS = 8
CONF_KERNEL = 31
CONF_GROUPS = 8
PEER_HEADS = 8
PEER_NKEYS = 128
KEY_BITS = 7
PEER_TOPK = 16
PEER_DKH = 64
N_SEL = PEER_HEADS * PEER_TOPK
MIX0_HALO = 16
MIX1_HALO = 32
TOKEN_UNROLL = 128


def _params(semantics, *buffer_bytes):
    limit = min(int(sum(buffer_bytes)) + COMPILER_VMEM_ALLOWANCE, V7X_VMEM_BYTES - 4 * 1024 * 1024)
    return pltpu.CompilerParams(dimension_semantics=semantics, vmem_limit_bytes=limit)


def _nbytes(shape, dtype, buffers=2):
    n = 1
    for s in shape:
        n *= s
    return n * jnp.dtype(dtype).itemsize * buffers


def _gelu(x):
    return 0.5 * x * (1.0 + lax.erf(x * (2.0 ** -0.5)))


def _tile(total, want):
    t = min(total, want)
    assert total % t == 0, (total, want)
    return t


def _cast_kernel(x_ref, o_ref):
    o_ref[...] = x_ref[...].astype(BF16)


def _layer_to_bf16(table, layer):
    _, rows, cols = table.shape
    tb = _tile(rows, 1024)
    return pl.pallas_call(
        _cast_kernel,
        out_shape=jax.ShapeDtypeStruct((rows, cols), BF16),
        grid=(rows // tb,),
        in_specs=[pl.BlockSpec((None, tb, cols), lambda i: (layer, i, 0))],
        out_specs=pl.BlockSpec((tb, cols), lambda i: (i, 0)),
        compiler_params=_params(("parallel",), _nbytes((tb, cols), F32), _nbytes((tb, cols), BF16)),
        name="table_to_bf16",
    )(table)


def _norm_mm_kernel(x_ref, g_ref, w_ref, o_ref, zn_ref):
    @pl.when(pl.program_id(1) == 0)
    def _():
        x = x_ref[...]
        ms = jnp.mean(x * x, axis=-1, keepdims=True)
        zn_ref[...] = (x * lax.rsqrt(ms + EPS) * g_ref[...]).astype(BF16)

    o_ref[...] = jnp.dot(zn_ref[...], w_ref[...], preferred_element_type=F32)


def _norm_matmul(x, g, w):
    t, d = x.shape
    n = w.shape[1]
    tb, tn = _tile(t, 1024), _tile(n, 2048)
    return pl.pallas_call(
        _norm_mm_kernel,
        out_shape=jax.ShapeDtypeStruct((t, n), F32),
        grid=(t // tb, n // tn),
        in_specs=[pl.BlockSpec((tb, d), lambda i, j: (i, 0)),
                  pl.BlockSpec((1, d), lambda i, j: (0, 0)),
                  pl.BlockSpec((d, tn), lambda i, j: (0, j))],
        out_specs=pl.BlockSpec((tb, tn), lambda i, j: (i, j)),
        scratch_shapes=[pltpu.VMEM((tb, d), BF16)],
        compiler_params=_params(("parallel", "arbitrary"), _nbytes((tb, d), F32), _nbytes((d, tn), BF16),
                                _nbytes((tb, tn), F32), _nbytes((tb, d), BF16, 1)),
        name="norm_matmul",
    )(x, g.reshape(1, d), w)


def _mm_res_kernel(y_ref, w_ref, h_ref, o_ref):
    o_ref[...] = h_ref[...] + jnp.dot(y_ref[...], w_ref[...], preferred_element_type=F32)


def _matmul_residual(y, w, h):
    t, k = y.shape
    n = w.shape[1]
    tb, tn = _tile(t, 512), _tile(n, 2048)
    return pl.pallas_call(
        _mm_res_kernel,
        out_shape=jax.ShapeDtypeStruct((t, n), F32),
        grid=(t // tb, n // tn),
        in_specs=[pl.BlockSpec((tb, k), lambda i, j: (i, 0)),
                  pl.BlockSpec((k, tn), lambda i, j: (0, j)),
                  pl.BlockSpec((tb, tn), lambda i, j: (i, j))],
        out_specs=pl.BlockSpec((tb, tn), lambda i, j: (i, j)),
        compiler_params=_params(("parallel", "parallel"), _nbytes((tb, k), BF16), _nbytes((k, tn), BF16),
                                _nbytes((tb, tn), F32), _nbytes((tb, tn), F32)),
        name="matmul_residual",
    )(y, w, h)


def _mix0_kernel(p_ref, halo_ref, pw_ref, ps_ref, cw_ref, y_ref, ext_ref, *, tbm, seq, d_a):
    row0 = pl.program_id(0) * tbm
    at_seq_start = (row0 % seq) == 0
    halo = MIX0_HALO
    gc = d_a // len(POOL_WINDOWS)

    ext_ref[0:halo, 0:d_a] = jnp.where(at_seq_start, 0.0, halo_ref[:, 0:d_a])
    ext_ref[0:halo, d_a:2 * d_a] = jnp.where(
        at_seq_start, 0.0, halo_ref[:, 2 * d_a:3 * d_a] * halo_ref[:, 3 * d_a:4 * d_a])
    ext_ref[halo:halo + tbm, 0:d_a] = p_ref[:, 0:d_a]
    ext_ref[halo:halo + tbm, d_a:2 * d_a] = p_ref[:, 2 * d_a:3 * d_a] * p_ref[:, 3 * d_a:4 * d_a]

    pos = (row0 % seq) + lax.broadcasted_iota(jnp.int32, (tbm, gc), 0)
    for g, win in enumerate(POOL_WINDOWS):
        cs = slice(g * gc, (g + 1) * gc)
        x = ext_ref[halo:halo + tbm, cs]
        s = x
        for dlt in range(1, win):
            s = s + ext_ref[halo - dlt:halo - dlt + tbm, cs]
        cnt = jnp.minimum(pos + 1, win).astype(F32)
        pooled = s / cnt - x
        mixed = jnp.dot(pooled.astype(BF16), pw_ref[g], preferred_element_type=F32)
        y_ref[:, cs] = (mixed * ps_ref[:, cs]).astype(BF16)

    for c0 in range(0, d_a, 4 * V7X_LANES):
        cs = slice(c0, c0 + 4 * V7X_LANES)
        es = slice(d_a + c0, d_a + c0 + 4 * V7X_LANES)
        conv = cw_ref[SHORT_CONV - 1:SHORT_CONV, cs] * ext_ref[halo:halo + tbm, es]
        for dlt in range(1, SHORT_CONV):
            k = SHORT_CONV - 1 - dlt
            conv = conv + cw_ref[k:k + 1, cs] * ext_ref[halo - dlt:halo - dlt + tbm, es]
        bg = p_ref[:, d_a + c0:d_a + c0 + 4 * V7X_LANES]
        y_ref[:, d_a + c0:d_a + c0 + 4 * V7X_LANES] = (bg * conv).astype(BF16)


def _mix0(p, pool_w, pool_scale, conv_w, *, seq):
    t, four_da = p.shape
    d_a = four_da // 4
    tbm = _tile(seq, 256)
    halo = MIX0_HALO
    gc = d_a // len(POOL_WINDOWS)
    kern = functools.partial(_mix0_kernel, tbm=tbm, seq=seq, d_a=d_a)
    return pl.pallas_call(
        kern,
        out_shape=jax.ShapeDtypeStruct((t, 2 * d_a), BF16),
        grid=(t // tbm,),
        in_specs=[pl.BlockSpec((tbm, four_da), lambda i: (i, 0)),
                  pl.BlockSpec((halo, four_da), lambda i: (jnp.maximum(i * (tbm // halo) - 1, 0), 0)),
                  pl.BlockSpec((len(POOL_WINDOWS), gc, gc), lambda i: (0, 0, 0)),
                  pl.BlockSpec((1, d_a), lambda i: (0, 0)),
                  pl.BlockSpec((SHORT_CONV, d_a), lambda i: (0, 0))],
        out_specs=pl.BlockSpec((tbm, 2 * d_a), lambda i: (i, 0)),
        scratch_shapes=[pltpu.VMEM((halo + tbm, 2 * d_a), F32)],
        compiler_params=_params(("parallel",), _nbytes((tbm, four_da), F32), _nbytes((halo, four_da), F32),
                                _nbytes((tbm, 2 * d_a), BF16), _nbytes((halo + tbm, 2 * d_a), F32, 1)),
        name="mix_even",
    )(p, p, pool_w, pool_scale.reshape(1, d_a), conv_w)


def _mix1_kernel(p_ref, halo_ref, lng_ref, lnb_ref, ws_ref, sb_ref, cw_ref, cb_ref, gng_ref, gnb_ref,
                 y_ref, vn_ref, zext_ref, shift_ref, *, tbm, seq, d_c):
    row0 = pl.program_id(0) * tbm
    at_seq_start = (row0 % seq) == 0
    halo = MIX1_HALO
    hd = d_c // SGU_HEADS

    v = _gelu(p_ref[:, d_c:2 * d_c])
    mu = jnp.mean(v, axis=-1, keepdims=True)
    vc = v - mu
    var = jnp.mean(vc * vc, axis=-1, keepdims=True)
    vn_ref[...] = ((vc * lax.rsqrt(var + EPS)) * lng_ref[...] + lnb_ref[...]).astype(BF16)

    r = lax.broadcasted_iota(jnp.int32, (SGU_CHUNK, SGU_CHUNK), 0)
    c = lax.broadcasted_iota(jnp.int32, (SGU_CHUNK, SGU_CHUNK), 1)
    for h in range(SGU_HEADS):
        cs = slice(h * hd, (h + 1) * hd)
        wm = jnp.where(r >= c, ws_ref[h], 0.0).astype(BF16)
        for ch in range(tbm // SGU_CHUNK):
            rs = slice(ch * SGU_CHUNK, (ch + 1) * SGU_CHUNK)
            mixed = jnp.dot(wm, vn_ref[rs, cs], preferred_element_type=F32) + sb_ref[:, cs]
            u = _gelu(p_ref[rs, cs])
            y_ref[rs, cs] = (u * mixed).astype(BF16)

    zext_ref[0:halo, :] = jnp.where(
        at_seq_start, 0.0, halo_ref[:, 2 * d_c:3 * d_c] * jax.nn.sigmoid(halo_ref[:, 3 * d_c:4 * d_c]))
    zext_ref[halo:halo + tbm, :] = p_ref[:, 2 * d_c:3 * d_c] * jax.nn.sigmoid(p_ref[:, 3 * d_c:4 * d_c])
    gd = d_c // CONF_GROUPS
    first = halo - (CONF_KERNEL - 1)
    for gi in range(CONF_GROUPS):
        cs = slice(gi * gd, (gi + 1) * gd)
        acc = jnp.broadcast_to(cb_ref[:, cs], (tbm, gd))
        for r in range(V7X_SUBLANES):
            taps = range(r, CONF_KERNEL, V7X_SUBLANES)
            rows = taps[-1] - r + tbm
            shift_ref[0:rows, :] = zext_ref[first + r:first + r + rows, cs]
            for k in taps:
                acc = acc + cw_ref[k:k + 1, cs] * shift_ref[k - r:k - r + tbm, :]
        mu = jnp.mean(acc, axis=-1, keepdims=True)
        zc = acc - mu
        var = jnp.mean(zc * zc, axis=-1, keepdims=True)
        zn = (zc * lax.rsqrt(var + EPS)) * gng_ref[:, cs] + gnb_ref[:, cs]
        y_ref[:, d_c + gi * gd:d_c + (gi + 1) * gd] = (zn * jax.nn.sigmoid(zn)).astype(BF16)


def _mix1(p, ln_g, ln_b, sgu_w, sgu_b, conv_w, conv_b, gn_g, gn_b, *, seq):
    t, four_dc = p.shape
    d_c = four_dc // 4
    tbm = _tile(seq, 256)
    halo = MIX1_HALO
    hd = d_c // SGU_HEADS
    sb = jnp.repeat(sgu_b.T, hd, axis=1)
    row = lambda a: a.reshape(1, d_c)
    kern = functools.partial(_mix1_kernel, tbm=tbm, seq=seq, d_c=d_c)
    const2 = lambda i: (0, 0)
    return pl.pallas_call(
        kern,
        out_shape=jax.ShapeDtypeStruct((t, 2 * d_c), BF16),
        grid=(t // tbm,),
        in_specs=[pl.BlockSpec((tbm, four_dc), lambda i: (i, 0)),
                  pl.BlockSpec((halo, four_dc), lambda i: (jnp.maximum(i * (tbm // halo) - 1, 0), 0)),
                  pl.BlockSpec((1, d_c), const2),
                  pl.BlockSpec((1, d_c), const2),
                  pl.BlockSpec((SGU_HEADS, SGU_CHUNK, SGU_CHUNK), lambda i: (0, 0, 0)),
                  pl.BlockSpec((SGU_CHUNK, d_c), const2),
                  pl.BlockSpec((CONF_KERNEL, d_c), const2),
                  pl.BlockSpec((1, d_c), const2),
                  pl.BlockSpec((1, d_c), const2),
                  pl.BlockSpec((1, d_c), const2)],
        out_specs=pl.BlockSpec((tbm, 2 * d_c), lambda i: (i, 0)),
        scratch_shapes=[pltpu.VMEM((tbm, d_c), BF16), pltpu.VMEM((halo + tbm, d_c), F32),
                        pltpu.VMEM((halo + tbm, d_c // CONF_GROUPS), F32)],
        compiler_params=_params(("parallel",), _nbytes((tbm, four_dc), F32), _nbytes((halo, four_dc), F32),
                                _nbytes((tbm, 2 * d_c), BF16), _nbytes((halo + tbm, d_c), F32, 1),
                                _nbytes((SGU_CHUNK, d_c), F32)),
        name="mix_odd",
    )(p, p, row(ln_g), row(ln_b), sgu_w, sb, conv_w, row(conv_b), row(gn_g), row(gn_b))


_NO_ID = float(2 ** 20)


def _sorted_top16(x, sub):
    ng = PEER_NKEYS // V7X_SUBLANES
    xs = [x[v * V7X_SUBLANES:(v + 1) * V7X_SUBLANES, :] for v in range(ng)]
    ids = [sub + float(v * V7X_SUBLANES) for v in range(ng)]
    for rnd in range(ng):
        for v in range(rnd % 2, ng - 1, 2):
            a, b, ia, ib = xs[v], xs[v + 1], ids[v], ids[v + 1]
            swap = b > a
            xs[v], xs[v + 1] = jnp.maximum(a, b), jnp.minimum(a, b)
            ids[v], ids[v + 1] = jnp.where(swap, ib, ia), jnp.where(swap, ia, ib)
    vals, picks = [], []
    for n in range(PEER_TOPK):
        m = jnp.max(xs[0], axis=0, keepdims=True)
        sel = jnp.min(jnp.where(xs[0] == m, ids[0], _NO_ID), axis=0, keepdims=True)
        vals.append(m)
        picks.append(sel)
        win = ids[0] == sel
        for v in range(PEER_TOPK - 1 - n):
            xs[v] = jnp.where(win, xs[v + 1], xs[v])
            ids[v] = jnp.where(win, ids[v + 1], ids[v])
    return vals, picks


def _pair_top16(s0, i0, s1, i1, sub):
    half = PEER_TOPK // 2
    s0_lo, s0_hi = jnp.concatenate(s0[:half], axis=0), jnp.concatenate(s0[half:], axis=0)
    e0_lo = jnp.concatenate(i0[:half], axis=0) * float(PEER_NKEYS)
    e0_hi = jnp.concatenate(i0[half:], axis=0) * float(PEER_NKEYS)
    slot_v, slot_e = [], []
    for l in range(PEER_TOPK):
        v = s0_lo + s1[l]
        rows_valid = PEER_TOPK // (l + 1)
        if rows_valid < half:
            v = jnp.where(sub < float(rows_valid), v, -jnp.inf)
        slot_v.append(v)
        slot_e.append(e0_lo + i1[l])
    hi_v, hi_e = s0_hi + s1[0], e0_hi + i1[0]
    pos_lo = sub * float(PEER_TOPK)
    pos_hi = (sub + float(half)) * float(PEER_TOPK)
    sums, experts = [], []
    for n in range(PEER_TOPK):
        m = jnp.max(jnp.maximum(slot_v[0], hi_v), axis=0, keepdims=True)
        pick = jnp.min(jnp.minimum(jnp.where(slot_v[0] == m, pos_lo, _NO_ID),
                                   jnp.where(hi_v == m, pos_hi, _NO_ID)), axis=0, keepdims=True)
        win_lo, win_hi = pos_lo == pick, pos_hi == pick
        experts.append(jnp.max(jnp.maximum(jnp.where(win_lo, slot_e[0], -1.0),
                                           jnp.where(win_hi, hi_e, -1.0)), axis=0, keepdims=True))
        sums.append(m)
        pos_lo = jnp.where(win_lo, pos_lo + 1.0, pos_lo)
        hi_v = jnp.where(win_hi, -jnp.inf, hi_v)
        for l in range(PEER_TOPK - 1 - n):
            slot_v[l] = jnp.where(win_lo, slot_v[l + 1], slot_v[l])
            slot_e[l] = jnp.where(win_lo, slot_e[l + 1], slot_e[l])
    return sums, experts


def _select_kernel(h_ref, g_ref, wq_ref, keys_ref, xn_ref, e_ref, gate_ref, qt_ref, fs_ref, fe_ref, *, tbs):
    x = h_ref[...]
    ms = jnp.mean(x * x, axis=-1, keepdims=True)
    xn = (x * lax.rsqrt(ms + EPS) * g_ref[...]).astype(BF16)
    xn_ref[...] = xn
    qt_ref[...] = lax.dot_general(wq_ref[...], xn, (((1,), (1,)), ((), ())),
                                  preferred_element_type=F32).astype(BF16)

    sub =lax.broadcasted_iota(jnp.int32, (V7X_SUBLANES, tbs), 0).astype(F32)

    def head(hd, carry):
        tops = []
        for part in range(2):
            hp = hd * 2 + part
            q = qt_ref[pl.ds(pl.multiple_of(hp * PEER_DKH, PEER_DKH), PEER_DKH), :]
            scores = jnp.dot(keys_ref[hp], q, preferred_element_type=F32)
            tops.append(_sorted_top16(scores, sub))
        (s0, i0), (s1, i1) = tops
        f_s, f_e = _pair_top16(s0, i0, s1, i1, sub)
        f_s = jnp.concatenate(f_s, axis=0)
        ex = jnp.exp(f_s - f_s[0:1])
        gates = ex / jnp.sum(ex, axis=0, keepdims=True)
        rows = pl.ds(pl.multiple_of(hd * PEER_TOPK, PEER_TOPK), PEER_TOPK)
        fs_ref[rows, :] = gates
        fe_ref[rows, :] = jnp.concatenate(f_e, axis=0)
        return carry

    lax.fori_loop(0, PEER_HEADS, head, 0, unroll=PEER_HEADS)
    gate_ref[...] = fs_ref[...].T
    e_ref[...] = fe_ref[...].T


def _select(h, g, wq_t, keys):
    t, d = h.shape
    tbs = _tile(t, 256)
    kern = functools.partial(_select_kernel, tbs=tbs)
    nq = wq_t.shape[0]
    return pl.pallas_call(
        kern,
        out_shape=(jax.ShapeDtypeStruct((t, d), BF16),
                   jax.ShapeDtypeStruct((t, N_SEL), F32),
                   jax.ShapeDtypeStruct((t, N_SEL), F32)),
        grid=(t // tbs,),
        in_specs=[pl.BlockSpec((tbs, d), lambda i: (i, 0)),
                  pl.BlockSpec((1, d), lambda i: (0, 0)),
                  pl.BlockSpec((nq, d), lambda i: (0, 0)),
                  pl.BlockSpec(keys.shape, lambda i: (0, 0, 0))],
        out_specs=(pl.BlockSpec((tbs, d), lambda i: (i, 0)),
                   pl.BlockSpec((tbs, N_SEL), lambda i: (i, 0)),
                   pl.BlockSpec((tbs, N_SEL), lambda i: (i, 0))),
        scratch_shapes=[pltpu.VMEM((nq, tbs), BF16),
                        pltpu.VMEM((N_SEL, tbs), F32),
                        pltpu.VMEM((N_SEL, tbs), F32)],
        compiler_params=_params(("parallel",), _nbytes((tbs, d), F32), _nbytes((nq, d), BF16),
                                _nbytes((tbs, d), BF16), _nbytes(keys.shape, BF16)),
        name="peer_select",
    )(h, g.reshape(1, d), wq_t, keys)


def _gates_kernel(e_ref, gate_ref, o_ref, tile_ref, *, tbg, pitch):
    sub_id = lax.broadcasted_iota(jnp.int32, (PEER_NKEYS, N_SEL), 0)

    def token(tk, carry):
        e = e_ref[pl.ds(tk, 1), :].astype(jnp.int32)
        gate = gate_ref[pl.ds(tk, 1), :]
        hot_i = jnp.where(sub_id == (e >> KEY_BITS), 1.0, 0.0).astype(BF16)
        hot_j = jnp.where(sub_id == (e & (PEER_NKEYS - 1)), gate, 0.0).astype(BF16)
        tile = lax.dot_general(hot_i, hot_j, (((1,), (1,)), ((), ())), preferred_element_type=F32)
        tile_ref[pl.ds(tk, PEER_NKEYS, stride=pitch), :] = tile
        return carry

    lax.fori_loop(0, tbg, token, 0, unroll=TOKEN_UNROLL)
    for i in range(PEER_NKEYS):
        o_ref[:, i * PEER_NKEYS:(i + 1) * PEER_NKEYS] = tile_ref[i * pitch:i * pitch + tbg, :].astype(BF16)


def _dense_gates(e, gate):
    t = e.shape[0]
    n_exp = PEER_NKEYS * PEER_NKEYS
    tbg = _tile(t, 128)
    pitch = tbg + V7X_SUBLANES
    kern = functools.partial(_gates_kernel, tbg=tbg, pitch=pitch)
    return pl.pallas_call(
        kern,
        out_shape=jax.ShapeDtypeStruct((t, n_exp), BF16),
        grid=(t // tbg,),
        in_specs=[pl.BlockSpec((tbg, N_SEL), lambda i: (i, 0)),
                  pl.BlockSpec((tbg, N_SEL), lambda i: (i, 0))],
        out_specs=pl.BlockSpec((tbg, n_exp), lambda i: (i, 0)),
        scratch_shapes=[pltpu.VMEM((PEER_NKEYS * pitch, PEER_NKEYS), F32)],
        compiler_params=_params(("parallel",), _nbytes((tbg, n_exp), BF16),
                                _nbytes((PEER_NKEYS * pitch, PEER_NKEYS), F32, 1)),
        name="peer_dense_gates",
    )(e, gate)


def _ffn_kernel(xn_ref, u_ref, v_ref, gt_ref, h_ref, fg_ref, o_ref, *, final_norm):
    n = pl.program_id(1)

    @pl.when(n == 0)
    def _():
        o_ref[...] = h_ref[...]

    s = lax.dot_general(xn_ref[...], u_ref[...], (((1,), (1,)), ((), ())), preferred_element_type=F32)
    act = (_gelu(s) * gt_ref[...].astype(F32)).astype(BF16)
    o_ref[...] += jnp.dot(act, v_ref[...], preferred_element_type=F32)

    if final_norm:
        @pl.when(n == pl.num_programs(1) - 1)
        def _():
            x = o_ref[...]
            ms = jnp.mean(x * x, axis=-1, keepdims=True)
            o_ref[...] = x * lax.rsqrt(ms + EPS) * fg_ref[...]


def _peer_ffn(xn, u, v, gt, h, final_g, *, final_norm):
    t, d = xn.shape
    n_exp = u.shape[0]
    tb, nb = _tile(t, 1024), _tile(n_exp, 1024)
    kern = functools.partial(_ffn_kernel, final_norm=final_norm)
    return pl.pallas_call(
        kern,
        out_shape=jax.ShapeDtypeStruct((t, d), F32),
        grid=(t // tb, n_exp // nb),
        in_specs=[pl.BlockSpec((tb, d), lambda i, n: (i, 0), pipeline_mode=pl.Buffered(1)),
                  pl.BlockSpec((nb, d), lambda i, n: (n, 0)),
                  pl.BlockSpec((nb, d), lambda i, n: (n, 0)),
                  pl.BlockSpec((tb, nb), lambda i, n: (i, n)),
                  pl.BlockSpec((tb, d), lambda i, n: (i, 0), pipeline_mode=pl.Buffered(1)),
                  pl.BlockSpec((1, d), lambda i, n: (0, 0))],
        out_specs=pl.BlockSpec((tb, d), lambda i, n: (i, 0)),
        compiler_params=_params(("parallel", "arbitrary"), _nbytes((tb, d), BF16, 1), 2 * _nbytes((nb, d), BF16),
                                _nbytes((tb, nb), BF16), _nbytes((tb, d), F32, 1), _nbytes((tb, d), F32),
                                _nbytes((tb, nb), F32, 1)),
        name="peer_ffn",
    )(xn, u, v, gt, h, final_g.reshape(1, d))


def kernel(x, mix_norm_g, ffn_norm_g, ev_w_in, ev_pool_w, ev_pool_scale, ev_conv_w, ev_w_out, od_w_in,
           od_v_norm_g, od_v_norm_b, od_sgu_w, od_sgu_b, od_conv_w, od_conv_b, od_gn_g, od_gn_b, od_w_out,
           peer_w_q, peer_subkeys, peer_u, peer_v, final_norm_g):
    bsz, seq, d = x.shape
    depth = mix_norm_g.shape[0]
    h = x.reshape(bsz * seq, d)
    for layer in range(depth):
        j = layer // 2
        if layer % 2 == 0:
            p = _norm_matmul(h, mix_norm_g[layer], ev_w_in[j].astype(BF16))
            y = _mix0(p, ev_pool_w[j].astype(BF16), ev_pool_scale[j], ev_conv_w[j], seq=seq)
            h = _matmul_residual(y, ev_w_out[j].astype(BF16), h)
        else:
            p = _norm_matmul(h, mix_norm_g[layer], od_w_in[j].astype(BF16))
            y = _mix1(p, od_v_norm_g[j], od_v_norm_b[j], od_sgu_w[j], od_sgu_b[j], od_conv_w[j],
                      od_conv_b[j], od_gn_g[j], od_gn_b[j], seq=seq)
            h = _matmul_residual(y, od_w_out[j].astype(BF16), h)
        keys = peer_subkeys[layer].reshape(2 * PEER_HEADS, PEER_NKEYS, PEER_DKH).astype(BF16)
        xn, e, gate = _select(h, ffn_norm_g[layer], peer_w_q[layer].T.astype(BF16), keys)
        gt = _dense_gates(e, gate)
        h = _peer_ffn(xn, _layer_to_bf16(peer_u, layer), _layer_to_bf16(peer_v, layer), gt, h, final_norm_g,
                      final_norm=(layer == depth - 1))
    return h.reshape(bsz, seq, d)
```

```python
import functools

import jax
import jax.numpy as jnp
from jax import lax
from jax.experimental import pallas as pl
from jax.experimental.pallas import tpu as pltpu

F32 = jnp.float32
BF16 = jnp.bfloat16
EPS = 1e-6

V7X_VMEM_BYTES = 64 * 1024 * 1024
V7X_LANES = 128
V7X_SUBLANES = 8
COMPILER_VMEM_ALLOWANCE = 12 * 1024 * 1024

POOL_WINDOWS = (2, 4, 8, 16)
SHORT_CONV = 3
SGU_CHUNK = 128
SGU_HEADS = 8
CONF_KERNEL = 31
CONF_GROUPS = 8
PEER_HEADS = 8
PEER_NKEYS = 128
KEY_BITS = 7
PEER_TOPK = 16
PEER_DKH = 64
N_SEL = PEER_HEADS * PEER_TOPK
MIX0_HALO = 16
MIX1_HALO = 32
TOKEN_UNROLL = 128


def _params(semantics, *buffer_bytes):
    limit = min(int(sum(buffer_bytes)) + COMPILER_VMEM_ALLOWANCE, V7X_VMEM_BYTES - 4 * 1024 * 1024)
    return pltpu.CompilerParams(dimension_semantics=semantics, vmem_limit_bytes=limit)


def _nbytes(shape, dtype, buffers=2):
    n = 1
    for s in shape:
        n *= s
    return n * jnp.dtype(dtype).itemsize * buffers


def _gelu(x):
    return 0.5 * x * (1.0 + lax.erf(x * (2.0 ** -0.5)))


def _tile(total, want):
    t = min(total, want)
    assert total % t == 0, (total, want)
    return t


def _cast_kernel(x_ref, o_ref):
    o_ref[...] = x_ref[...].astype(BF16)


def _layer_to_bf16(table, layer):
    _, rows, cols = table.shape
    tb = _tile(rows, 1024)
    return pl.pallas_call(
        _cast_kernel,
        out_shape=jax.ShapeDtypeStruct((rows, cols), BF16),
        grid=(rows // tb,),
        in_specs=[pl.BlockSpec((None, tb, cols), lambda i: (layer, i, 0))],
        out_specs=pl.BlockSpec((tb, cols), lambda i: (i, 0)),
        compiler_params=_params(("parallel",), _nbytes((tb, cols), F32), _nbytes((tb, cols), BF16)),
        name="table_to_bf16",
    )(table)


def _norm_mm_kernel(x_ref, g_ref, w_ref, o_ref, zn_ref):
    @pl.when(pl.program_id(1) == 0)
    def _():
        x = x_ref[...]
        ms = jnp.mean(x * x, axis=-1, keepdims=True)
        zn_ref[...] = (x * lax.rsqrt(ms + EPS) * g_ref[...]).astype(BF16)

    o_ref[...] = jnp.dot(zn_ref[...], w_ref[...], preferred_element_type=F32)


def _norm_matmul(x, g, w):
    t, d = x.shape
    n = w.shape[1]
    tb, tn = _tile(t, 1024), _tile(n, 2048)
    return pl.pallas_call(
        _norm_mm_kernel,
        out_shape=jax.ShapeDtypeStruct((t, n), F32),
        grid=(t // tb, n // tn),
        in_specs=[pl.BlockSpec((tb, d), lambda i, j: (i, 0)),
                  pl.BlockSpec((1, d), lambda i, j: (0, 0)),
                  pl.BlockSpec((d, tn), lambda i, j: (0, j))],
        out_specs=pl.BlockSpec((tb, tn), lambda i, j: (i, j)),
        scratch_shapes=[pltpu.VMEM((tb, d), BF16)],
        compiler_params=_params(("parallel", "arbitrary"), _nbytes((tb, d), F32), _nbytes((d, tn), BF16),
                                _nbytes((tb, tn), F32), _nbytes((tb, d), BF16, 1)),
        name="norm_matmul",
    )(x, g.reshape(1, d), w)


def _mm_res_kernel(y_ref, w_ref, h_ref, o_ref):
    o_ref[...] = h_ref[...] + jnp.dot(y_ref[...], w_ref[...], preferred_element_type=F32)


def _matmul_residual(y, w, h):
    t, k = y.shape
    n = w.shape[1]
    tb, tn = _tile(t, 512), _tile(n, 2048)
    return pl.pallas_call(
        _mm_res_kernel,
        out_shape=jax.ShapeDtypeStruct((t, n), F32),
        grid=(t // tb, n // tn),
        in_specs=[pl.BlockSpec((tb, k), lambda i, j: (i, 0)),
                  pl.BlockSpec((k, tn), lambda i, j: (0, j)),
                  pl.BlockSpec((tb, tn), lambda i, j: (i, j))],
        out_specs=pl.BlockSpec((tb, tn), lambda i, j: (i, j)),
        compiler_params=_params(("parallel", "parallel"), _nbytes((tb, k), BF16), _nbytes((k, tn), BF16),
                                _nbytes((tb, tn), F32), _nbytes((tb, tn), F32)),
        name="matmul_residual",
    )(y, w, h)


def _mix0_kernel(p_ref, halo_ref, pw_ref, ps_ref, cw_ref, y_ref, ext_ref, *, tbm, seq, d_a):
    row0 = pl.program_id(0) * tbm
    at_seq_start = (row0 % seq) == 0
    halo = MIX0_HALO
    gc = d_a // len(POOL_WINDOWS)

    ext_ref[0:halo, 0:d_a] = jnp.where(at_seq_start, 0.0, halo_ref[:, 0:d_a])
    ext_ref[0:halo, d_a:2 * d_a] = jnp.where(
        at_seq_start, 0.0, halo_ref[:, 2 * d_a:3 * d_a] * halo_ref[:, 3 * d_a:4 * d_a])
    ext_ref[halo:halo + tbm, 0:d_a] = p_ref[:, 0:d_a]
    ext_ref[halo:halo + tbm, d_a:2 * d_a] = p_ref[:, 2 * d_a:3 * d_a] * p_ref[:, 3 * d_a:4 * d_a]

    pos = (row0 % seq) + lax.broadcasted_iota(jnp.int32, (tbm, gc), 0)
    for g, win in enumerate(POOL_WINDOWS):
        cs = slice(g * gc, (g + 1) * gc)
        x = ext_ref[halo:halo + tbm, cs]
        s = x
        for dlt in range(1, win):
            s = s + ext_ref[halo - dlt:halo - dlt + tbm, cs]
        cnt = jnp.minimum(pos + 1, win).astype(F32)
        pooled = s / cnt - x
        mixed = jnp.dot(pooled.astype(BF16), pw_ref[g], preferred_element_type=F32)
        y_ref[:, cs] = (mixed * ps_ref[:, cs]).astype(BF16)

    for c0 in range(0, d_a, 4 * V7X_LANES):
        cs = slice(c0, c0 + 4 * V7X_LANES)
        es = slice(d_a + c0, d_a + c0 + 4 * V7X_LANES)
        conv = cw_ref[SHORT_CONV - 1:SHORT_CONV, cs] * ext_ref[halo:halo + tbm, es]
        for dlt in range(1, SHORT_CONV):
            k = SHORT_CONV - 1 - dlt
            conv = conv + cw_ref[k:k + 1, cs] * ext_ref[halo - dlt:halo - dlt + tbm, es]
        bg = p_ref[:, d_a + c0:d_a + c0 + 4 * V7X_LANES]
        y_ref[:, d_a + c0:d_a + c0 + 4 * V7X_LANES] = (bg * conv).astype(BF16)


def _mix0(p, pool_w, pool_scale, conv_w, *, seq):
    t, four_da = p.shape
    d_a = four_da // 4
    tbm = _tile(seq, 256)
    halo = MIX0_HALO
    gc = d_a // len(POOL_WINDOWS)
    kern = functools.partial(_mix0_kernel, tbm=tbm, seq=seq, d_a=d_a)
    return pl.pallas_call(
        kern,
        out_shape=jax.ShapeDtypeStruct((t, 2 * d_a), BF16),
        grid=(t // tbm,),
        in_specs=[pl.BlockSpec((tbm, four_da), lambda i: (i, 0)),
                  pl.BlockSpec((halo, four_da), lambda i: (jnp.maximum(i * (tbm // halo) - 1, 0), 0)),
                  pl.BlockSpec((len(POOL_WINDOWS), gc, gc), lambda i: (0, 0, 0)),
                  pl.BlockSpec((1, d_a), lambda i: (0, 0)),
                  pl.BlockSpec((SHORT_CONV, d_a), lambda i: (0, 0))],
        out_specs=pl.BlockSpec((tbm, 2 * d_a), lambda i: (i, 0)),
        scratch_shapes=[pltpu.VMEM((halo + tbm, 2 * d_a), F32)],
        compiler_params=_params(("parallel",), _nbytes((tbm, four_da), F32), _nbytes((halo, four_da), F32),
                                _nbytes((tbm, 2 * d_a), BF16), _nbytes((halo + tbm, 2 * d_a), F32, 1)),
        name="mix_even",
    )(p, p, pool_w, pool_scale.reshape(1, d_a), conv_w)


def _mix1_kernel(p_ref, halo_ref, lng_ref, lnb_ref, ws_ref, sb_ref, cw_ref, cb_ref, gng_ref, gnb_ref,
                 y_ref, vn_ref, zext_ref, shift_ref, *, tbm, seq, d_c):
    row0 = pl.program_id(0) * tbm
    at_seq_start = (row0 % seq) == 0
    halo = MIX1_HALO
    hd = d_c // SGU_HEADS

    v = _gelu(p_ref[:, d_c:2 * d_c])
    mu = jnp.mean(v, axis=-1, keepdims=True)
    vc = v - mu
    var = jnp.mean(vc * vc, axis=-1, keepdims=True)
    vn_ref[...] = ((vc * lax.rsqrt(var + EPS)) * lng_ref[...] + lnb_ref[...]).astype(BF16)

    r = lax.broadcasted_iota(jnp.int32, (SGU_CHUNK, SGU_CHUNK), 0)
    c = lax.broadcasted_iota(jnp.int32, (SGU_CHUNK, SGU_CHUNK), 1)
    for h in range(SGU_HEADS):
        cs = slice(h * hd, (h + 1) * hd)
        wm = jnp.where(r >= c, ws_ref[h], 0.0).astype(BF16)
        for ch in range(tbm // SGU_CHUNK):
            rs = slice(ch * SGU_CHUNK, (ch + 1) * SGU_CHUNK)
            mixed = jnp.dot(wm, vn_ref[rs, cs], preferred_element_type=F32) + sb_ref[:, cs]
            u = _gelu(p_ref[rs, cs])
            y_ref[rs, cs] = (u * mixed).astype(BF16)

    zext_ref[0:halo, :] = jnp.where(
        at_seq_start, 0.0, halo_ref[:, 2 * d_c:3 * d_c] * jax.nn.sigmoid(halo_ref[:, 3 * d_c:4 * d_c]))
    zext_ref[halo:halo + tbm, :] = p_ref[:, 2 * d_c:3 * d_c] * jax.nn.sigmoid(p_ref[:, 3 * d_c:4 * d_c])
    gd = d_c // CONF_GROUPS
    first = halo - (CONF_KERNEL - 1)
    for gi in range(CONF_GROUPS):
        cs = slice(gi * gd, (gi + 1) * gd)
        acc = jnp.broadcast_to(cb_ref[:, cs], (tbm, gd))
        for r in range(V7X_SUBLANES):
            taps = range(r, CONF_KERNEL, V7X_SUBLANES)
            rows = taps[-1] - r + tbm
            shift_ref[0:rows, :] = zext_ref[first + r:first + r + rows, cs]
            for k in taps:
                acc = acc + cw_ref[k:k + 1, cs] * shift_ref[k - r:k - r + tbm, :]
        mu = jnp.mean(acc, axis=-1, keepdims=True)
        zc = acc - mu
        var = jnp.mean(zc * zc, axis=-1, keepdims=True)
        zn = (zc * lax.rsqrt(var + EPS)) * gng_ref[:, cs] + gnb_ref[:, cs]
        y_ref[:, d_c + gi * gd:d_c + (gi + 1) * gd] = (zn * jax.nn.sigmoid(zn)).astype(BF16)


def _mix1(p, ln_g, ln_b, sgu_w, sgu_b, conv_w, conv_b, gn_g, gn_b, *, seq):
    t, four_dc = p.shape
    d_c = four_dc // 4
    tbm = _tile(seq, 256)
    halo = MIX1_HALO
    hd = d_c // SGU_HEADS
    sb = jnp.repeat(sgu_b.T, hd, axis=1)
    row = lambda a: a.reshape(1, d_c)
    kern = functools.partial(_mix1_kernel, tbm=tbm, seq=seq, d_c=d_c)
    const2 = lambda i: (0, 0)
    return pl.pallas_call(
        kern,
        out_shape=jax.ShapeDtypeStruct((t, 2 * d_c), BF16),
        grid=(t // tbm,),
        in_specs=[pl.BlockSpec((tbm, four_dc), lambda i: (i, 0)),
                  pl.BlockSpec((halo, four_dc), lambda i: (jnp.maximum(i * (tbm // halo) - 1, 0), 0)),
                  pl.BlockSpec((1, d_c), const2),
                  pl.BlockSpec((1, d_c), const2),
                  pl.BlockSpec((SGU_HEADS, SGU_CHUNK, SGU_CHUNK), lambda i: (0, 0, 0)),
                  pl.BlockSpec((SGU_CHUNK, d_c), const2),
                  pl.BlockSpec((CONF_KERNEL, d_c), const2),
                  pl.BlockSpec((1, d_c), const2),
                  pl.BlockSpec((1, d_c), const2),
                  pl.BlockSpec((1, d_c), const2)],
        out_specs=pl.BlockSpec((tbm, 2 * d_c), lambda i: (i, 0)),
        scratch_shapes=[pltpu.VMEM((tbm, d_c), BF16), pltpu.VMEM((halo + tbm, d_c), F32),
                        pltpu.VMEM((halo + tbm, d_c // CONF_GROUPS), F32)],
        compiler_params=_params(("parallel",), _nbytes((tbm, four_dc), F32), _nbytes((halo, four_dc), F32),
                                _nbytes((tbm, 2 * d_c), BF16), _nbytes((halo + tbm, d_c), F32, 1),
                                _nbytes((SGU_CHUNK, d_c), F32)),
        name="mix_odd",
    )(p, p, row(ln_g), row(ln_b), sgu_w, sb, conv_w, row(conv_b), row(gn_g), row(gn_b))


_NO_ID = float(2 ** 20)


def _sorted_top16(x, sub):
    ng = PEER_NKEYS // V7X_SUBLANES
    xs = [x[v * V7X_SUBLANES:(v + 1) * V7X_SUBLANES, :] for v in range(ng)]
    ids = [sub + float(v * V7X_SUBLANES) for v in range(ng)]
    for rnd in range(ng):
        for v in range(rnd % 2, ng - 1, 2):
            a, b, ia, ib = xs[v], xs[v + 1], ids[v], ids[v + 1]
            swap = b > a
            xs[v], xs[v + 1] = jnp.maximum(a, b), jnp.minimum(a, b)
            ids[v], ids[v + 1] = jnp.where(swap, ib, ia), jnp.where(swap, ia, ib)
    vals, picks = [], []
    for n in range(PEER_TOPK):
        m = jnp.max(xs[0], axis=0, keepdims=True)
        sel = jnp.min(jnp.where(xs[0] == m, ids[0], _NO_ID), axis=0, keepdims=True)
        vals.append(m)
        picks.append(sel)
        win = ids[0] == sel
        for v in range(PEER_TOPK - 1 - n):
            xs[v] = jnp.where(win, xs[v + 1], xs[v])
            ids[v] = jnp.where(win, ids[v + 1], ids[v])
    return vals, picks


def _pair_top16(s0, i0, s1, i1, sub):
    half = PEER_TOPK // 2
    s0_lo, s0_hi = jnp.concatenate(s0[:half], axis=0), jnp.concatenate(s0[half:], axis=0)
    e0_lo = jnp.concatenate(i0[:half], axis=0) * float(PEER_NKEYS)
    e0_hi = jnp.concatenate(i0[half:], axis=0) * float(PEER_NKEYS)
    slot_v, slot_e = [], []
    for l in range(PEER_TOPK):
        v = s0_lo + s1[l]
        rows_valid = PEER_TOPK // (l + 1)
        if rows_valid < half:
            v = jnp.where(sub < float(rows_valid), v, -jnp.inf)
        slot_v.append(v)
        slot_e.append(e0_lo + i1[l])
    hi_v, hi_e = s0_hi + s1[0], e0_hi + i1[0]
    pos_lo = sub * float(PEER_TOPK)
    pos_hi = (sub + float(half)) * float(PEER_TOPK)
    sums, experts = [], []
    for n in range(PEER_TOPK):
        m = jnp.max(jnp.maximum(slot_v[0], hi_v), axis=0, keepdims=True)
        pick = jnp.min(jnp.minimum(jnp.where(slot_v[0] == m, pos_lo, _NO_ID),
                                   jnp.where(hi_v == m, pos_hi, _NO_ID)), axis=0, keepdims=True)
        win_lo, win_hi = pos_lo == pick, pos_hi == pick
        experts.append(jnp.max(jnp.maximum(jnp.where(win_lo, slot_e[0], -1.0),
                                           jnp.where(win_hi, hi_e, -1.0)), axis=0, keepdims=True))
        sums.append(m)
        pos_lo = jnp.where(win_lo, pos_lo + 1.0, pos_lo)
        hi_v = jnp.where(win_hi, -jnp.inf, hi_v)
        for l in range(PEER_TOPK - 1 - n):
            slot_v[l] = jnp.where(win_lo, slot_v[l + 1], slot_v[l])
            slot_e[l] = jnp.where(win_lo, slot_e[l + 1], slot_e[l])
    return sums, experts


def _select_kernel(h_ref, g_ref, wq_ref, keys_ref, xn_ref, e_ref, gate_ref, qt_ref, fs_ref, fe_ref, *, tbs):
    x = h_ref[...]
    ms = jnp.mean(x * x, axis=-1, keepdims=True)
    xn = (x * lax.rsqrt(ms + EPS) * g_ref[...]).astype(BF16)
    xn_ref[...] = xn
    qt_ref[...] = lax.dot_general(wq_ref[...], xn, (((1,), (1,)), ((), ())),
                                  preferred_element_type=F32).astype(BF16)

    sub =lax.broadcasted_iota(jnp.int32, (V7X_SUBLANES, tbs), 0).astype(F32)

    def head(hd, carry):
        tops = []
        for part in range(2):
            hp = hd * 2 + part
            q = qt_ref[pl.ds(pl.multiple_of(hp * PEER_DKH, PEER_DKH), PEER_DKH), :]
            scores = jnp.dot(keys_ref[hp], q, preferred_element_type=F32)
            tops.append(_sorted_top16(scores, sub))
        (s0, i0), (s1, i1) = tops
        f_s, f_e = _pair_top16(s0, i0, s1, i1, sub)
        f_s = jnp.concatenate(f_s, axis=0)
        ex = jnp.exp(f_s - f_s[0:1])
        gates = ex / jnp.sum(ex, axis=0, keepdims=True)
        rows = pl.ds(pl.multiple_of(hd * PEER_TOPK, PEER_TOPK), PEER_TOPK)
        fs_ref[rows, :] = gates
        fe_ref[rows, :] = jnp.concatenate(f_e, axis=0)
        return carry

    lax.fori_loop(0, PEER_HEADS, head, 0, unroll=PEER_HEADS)
    gate_ref[...] = fs_ref[...].T
    e_ref[...] = fe_ref[...].T


def _select(h, g, wq_t, keys):
    t, d = h.shape
    tbs = _tile(t, 512)
    kern = functools.partial(_select_kernel, tbs=tbs)
    nq = wq_t.shape[0]
    return pl.pallas_call(
        kern,
        out_shape=(jax.ShapeDtypeStruct((t, d), BF16),
                   jax.ShapeDtypeStruct((t, N_SEL), F32),
                   jax.ShapeDtypeStruct((t, N_SEL), F32)),
        grid=(t // tbs,),
        in_specs=[pl.BlockSpec((tbs, d), lambda i: (i, 0)),
                  pl.BlockSpec((1, d), lambda i: (0, 0)),
                  pl.BlockSpec((nq, d), lambda i: (0, 0)),
                  pl.BlockSpec(keys.shape, lambda i: (0, 0, 0))],
        out_specs=(pl.BlockSpec((tbs, d), lambda i: (i, 0)),
                   pl.BlockSpec((tbs, N_SEL), lambda i: (i, 0)),
                   pl.BlockSpec((tbs, N_SEL), lambda i: (i, 0))),
        scratch_shapes=[pltpu.VMEM((nq, tbs), BF16),
                        pltpu.VMEM((N_SEL, tbs), F32),
                        pltpu.VMEM((N_SEL, tbs), F32)],
        compiler_params=_params(("parallel",), _nbytes((tbs, d), F32), _nbytes((nq, d), BF16),
                                _nbytes((tbs, d), BF16), _nbytes(keys.shape, BF16)),
        name="peer_select",
    )(h, g.reshape(1, d), wq_t, keys)


def _gates_kernel(e_ref, gate_ref, o_ref, tile_ref, *, tbg, pitch):
    sub_id = lax.broadcasted_iota(jnp.int32, (PEER_NKEYS, N_SEL), 0)

    def token(tk, carry):
        e = e_ref[pl.ds(tk, 1), :].astype(jnp.int32)
        gate = gate_ref[pl.ds(tk, 1), :]
        hot_i = jnp.where(sub_id == (e >> KEY_BITS), 1.0, 0.0).astype(BF16)
        hot_j = jnp.where(sub_id == (e & (PEER_NKEYS - 1)), gate, 0.0).astype(BF16)
        tile = lax.dot_general(hot_i, hot_j, (((1,), (1,)), ((), ())), preferred_element_type=F32)
        tile_ref[pl.ds(tk, PEER_NKEYS, stride=pitch), :] = tile
        return carry

    lax.fori_loop(0, tbg, token, 0, unroll=TOKEN_UNROLL)
    for i in range(PEER_NKEYS):
        o_ref[:, i * PEER_NKEYS:(i + 1) * PEER_NKEYS] = tile_ref[i * pitch:i * pitch + tbg, :].astype(BF16)


def _dense_gates(e, gate):
    t = e.shape[0]
    n_exp = PEER_NKEYS * PEER_NKEYS
    tbg = _tile(t, 128)
    pitch = tbg + V7X_SUBLANES
    kern = functools.partial(_gates_kernel, tbg=tbg, pitch=pitch)
    return pl.pallas_call(
        kern,
        out_shape=jax.ShapeDtypeStruct((t, n_exp), BF16),
        grid=(t // tbg,),
        in_specs=[pl.BlockSpec((tbg, N_SEL), lambda i: (i, 0)),
                  pl.BlockSpec((tbg, N_SEL), lambda i: (i, 0))],
        out_specs=pl.BlockSpec((tbg, n_exp), lambda i: (i, 0)),
        scratch_shapes=[pltpu.VMEM((PEER_NKEYS * pitch, PEER_NKEYS), F32)],
        compiler_params=_params(("parallel",), _nbytes((tbg, n_exp), BF16),
                                _nbytes((PEER_NKEYS * pitch, PEER_NKEYS), F32, 1)),
        name="peer_dense_gates",
    )(e, gate)


def _ffn_kernel(xn_ref, u_ref, v_ref, gt_ref, h_ref, fg_ref, o_ref, *, final_norm):
    n = pl.program_id(1)

    @pl.when(n == 0)
    def _():
        o_ref[...] = h_ref[...]

    s = lax.dot_general(xn_ref[...], u_ref[...], (((1,), (1,)), ((), ())), preferred_element_type=F32)
    act = (_gelu(s) * gt_ref[...].astype(F32)).astype(BF16)
    o_ref[...] += jnp.dot(act, v_ref[...], preferred_element_type=F32)

    if final_norm:
        @pl.when(n == pl.num_programs(1) - 1)
        def _():
            x = o_ref[...]
            ms = jnp.mean(x * x, axis=-1, keepdims=True)
            o_ref[...] = x * lax.rsqrt(ms + EPS) * fg_ref[...]


def _peer_ffn(xn, u, v, gt, h, final_g, *, final_norm):
    t, d = xn.shape
    n_exp = u.shape[0]
    tb, nb = _tile(t, 1024), _tile(n_exp, 1024)
    kern = functools.partial(_ffn_kernel, final_norm=final_norm)
    return pl.pallas_call(
        kern,
        out_shape=jax.ShapeDtypeStruct((t, d), F32),
        grid=(t // tb, n_exp // nb),
        in_specs=[pl.BlockSpec((tb, d), lambda i, n: (i, 0), pipeline_mode=pl.Buffered(1)),
                  pl.BlockSpec((nb, d), lambda i, n: (n, 0)),
                  pl.BlockSpec((nb, d), lambda i, n: (n, 0)),
                  pl.BlockSpec((tb, nb), lambda i, n: (i, n)),
                  pl.BlockSpec((tb, d), lambda i, n: (i, 0), pipeline_mode=pl.Buffered(1)),
                  pl.BlockSpec((1, d), lambda i, n: (0, 0))],
        out_specs=pl.BlockSpec((tb, d), lambda i, n: (i, 0)),
        compiler_params=_params(("parallel", "arbitrary"), _nbytes((tb, d), BF16, 1), 2 * _nbytes((nb, d), BF16),
                                _nbytes((tb, nb), BF16), _nbytes((tb, d), F32, 1), _nbytes((tb, d), F32),
                                _nbytes((tb, nb), F32, 1)),
        name="peer_ffn",
    )(xn, u, v, gt, h, final_g.reshape(1, d))


def kernel(x, mix_norm_g, ffn_norm_g, ev_w_in, ev_pool_w, ev_pool_scale, ev_conv_w, ev_w_out, od_w_in,
           od_v_norm_g, od_v_norm_b, od_sgu_w, od_sgu_b, od_conv_w, od_conv_b, od_gn_g, od_gn_b, od_w_out,
           peer_w_q, peer_subkeys, peer_u, peer_v, final_norm_g):
    bsz, seq, d = x.shape
    depth = mix_norm_g.shape[0]
    h = x.reshape(bsz * seq, d)
    for layer in range(depth):
        j = layer // 2
        if layer % 2 == 0:
            p = _norm_matmul(h, mix_norm_g[layer], ev_w_in[j].astype(BF16))
            y = _mix0(p, ev_pool_w[j].astype(BF16), ev_pool_scale[j], ev_conv_w[j], seq=seq)
            h = _matmul_residual(y, ev_w_out[j].astype(BF16), h)
        else:
            p = _norm_matmul(h, mix_norm_g[layer], od_w_in[j].astype(BF16))
            y = _mix1(p, od_v_norm_g[j], od_v_norm_b[j], od_sgu_w[j], od_sgu_b[j], od_conv_w[j],
                      od_conv_b[j], od_gn_g[j], od_gn_b[j], seq=seq)
            h = _matmul_residual(y, od_w_out[j].astype(BF16), h)
        keys = peer_subkeys[layer].reshape(2 * PEER_HEADS, PEER_NKEYS, PEER_DKH).astype(BF16)
        xn, e, gate = _select(h, ffn_norm_g[layer], peer_w_q[layer].T.astype(BF16), keys)
        gt = _dense_gates(e, gate)
        h = _peer_ffn(xn, _layer_to_bf16(peer_u, layer), _layer_to_bf16(peer_v, layer), gt, h, final_norm_g,
                      final_norm=(layer == depth - 1))
    return h.reshape(bsz, seq, d)
```

```python
import functools

import jax
import jax.numpy as jnp
from jax import lax
from jax.experimental import pallas as pl
from jax.experimental.pallas import tpu as pltpu

F32 = jnp.float32
BF16 = jnp.bfloat16
EPS = 1e-6

V7X_VMEM_BYTES = 64 * 1024 * 1024
V7X_LANES = 128
V7X_SUBLANES = 8
COMPILER_VMEM_ALLOWANCE = 12 * 1024 * 1024

POOL_WINDOWS = (2, 4, 8, 16)
SHORT_CONV = 3
SGU_CHUNK = 128
SGU_HEADS = 8
CONF_KERNEL = 31
CONF_GROUPS = 8
PEER_HEADS = 8
PEER_NKEYS = 128
KEY_BITS = 7
PEER_TOPK = 16
PEER_DKH = 64
N_SEL = PEER_HEADS * PEER_TOPK
MIX0_HALO = 16
MIX1_HALO = 32
TOKEN_UNROLL = 128


def _params(semantics, *buffer_bytes):
    limit = min(int(sum(buffer_bytes)) + COMPILER_VMEM_ALLOWANCE, V7X_VMEM_BYTES - 4 * 1024 * 1024)
    return pltpu.CompilerParams(dimension_semantics=semantics, vmem_limit_bytes=limit)


def _nbytes(shape, dtype, buffers=2):
    n = 1
    for s in shape:
        n *= s
    return n * jnp.dtype(dtype).itemsize * buffers


def _gelu(x):
    return 0.5 * x * (1.0 + lax.erf(x * (2.0 ** -0.5)))


def _tile(total, want):
    t = min(total, want)
    assert total % t == 0, (total, want)
    return t


def _cast_kernel(x_ref, o_ref):
    o_ref[...] = x_ref[...].astype(BF16)


def _layer_to_bf16(table, layer):
    _, rows, cols = table.shape
    tb = _tile(rows, 1024)
    return pl.pallas_call(
        _cast_kernel,
        out_shape=jax.ShapeDtypeStruct((rows, cols), BF16),
        grid=(rows // tb,),
        in_specs=[pl.BlockSpec((None, tb, cols), lambda i: (layer, i, 0))],
        out_specs=pl.BlockSpec((tb, cols), lambda i: (i, 0)),
        compiler_params=_params(("parallel",), _nbytes((tb, cols), F32), _nbytes((tb, cols), BF16)),
        name="table_to_bf16",
    )(table)


def _norm_mm_kernel(x_ref, g_ref, w_ref, o_ref, zn_ref):
    @pl.when(pl.program_id(1) == 0)
    def _():
        x = x_ref[...]
        ms = jnp.mean(x * x, axis=-1, keepdims=True)
        zn_ref[...] = (x * lax.rsqrt(ms + EPS) * g_ref[...]).astype(BF16)

    o_ref[...] = jnp.dot(zn_ref[...], w_ref[...], preferred_element_type=F32)


def _norm_matmul(x, g, w):
    t, d = x.shape
    n = w.shape[1]
    tb, tn = _tile(t, 1024), _tile(n, 2048)
    return pl.pallas_call(
        _norm_mm_kernel,
        out_shape=jax.ShapeDtypeStruct((t, n), F32),
        grid=(t // tb, n // tn),
        in_specs=[pl.BlockSpec((tb, d), lambda i, j: (i, 0)),
                  pl.BlockSpec((1, d), lambda i, j: (0, 0)),
                  pl.BlockSpec((d, tn), lambda i, j: (0, j))],
        out_specs=pl.BlockSpec((tb, tn), lambda i, j: (i, j)),
        scratch_shapes=[pltpu.VMEM((tb, d), BF16)],
        compiler_params=_params(("parallel", "arbitrary"), _nbytes((tb, d), F32), _nbytes((d, tn), BF16),
                                _nbytes((tb, tn), F32), _nbytes((tb, d), BF16, 1)),
        name="norm_matmul",
    )(x, g.reshape(1, d), w)


def _mm_res_kernel(y_ref, w_ref, h_ref, o_ref):
    o_ref[...] = h_ref[...] + jnp.dot(y_ref[...], w_ref[...], preferred_element_type=F32)


def _matmul_residual(y, w, h):
    t, k = y.shape
    n = w.shape[1]
    tb, tn = _tile(t, 512), _tile(n, 2048)
    return pl.pallas_call(
        _mm_res_kernel,
        out_shape=jax.ShapeDtypeStruct((t, n), F32),
        grid=(t // tb, n // tn),
        in_specs=[pl.BlockSpec((tb, k), lambda i, j: (i, 0)),
                  pl.BlockSpec((k, tn), lambda i, j: (0, j)),
                  pl.BlockSpec((tb, tn), lambda i, j: (i, j))],
        out_specs=pl.BlockSpec((tb, tn), lambda i, j: (i, j)),
        compiler_params=_params(("parallel", "parallel"), _nbytes((tb, k), BF16), _nbytes((k, tn), BF16),
                                _nbytes((tb, tn), F32), _nbytes((tb, tn), F32)),
        name="matmul_residual",
    )(y, w, h)


def _mix0_kernel(p_ref, halo_ref, pw_ref, ps_ref, cw_ref, y_ref, ext_ref, *, tbm, seq, d_a):
    row0 = pl.program_id(0) * tbm
    at_seq_start = (row0 % seq) == 0
    halo = MIX0_HALO
    gc = d_a // len(POOL_WINDOWS)

    ext_ref[0:halo, 0:d_a] = jnp.where(at_seq_start, 0.0, halo_ref[:, 0:d_a])
    ext_ref[0:halo, d_a:2 * d_a] = jnp.where(
        at_seq_start, 0.0, halo_ref[:, 2 * d_a:3 * d_a] * halo_ref[:, 3 * d_a:4 * d_a])
    ext_ref[halo:halo + tbm, 0:d_a] = p_ref[:, 0:d_a]
    ext_ref[halo:halo + tbm, d_a:2 * d_a] = p_ref[:, 2 * d_a:3 * d_a] * p_ref[:, 3 * d_a:4 * d_a]

    pos = (row0 % seq) + lax.broadcasted_iota(jnp.int32, (tbm, gc), 0)
    for g, win in enumerate(POOL_WINDOWS):
        cs = slice(g * gc, (g + 1) * gc)
        x = ext_ref[halo:halo + tbm, cs]
        s = x
        for dlt in range(1, win):
            s = s + ext_ref[halo - dlt:halo - dlt + tbm, cs]
        cnt = jnp.minimum(pos + 1, win).astype(F32)
        pooled = s / cnt - x
        mixed = jnp.dot(pooled.astype(BF16), pw_ref[g], preferred_element_type=F32)
        y_ref[:, cs] = (mixed * ps_ref[:, cs]).astype(BF16)

    for c0 in range(0, d_a, 4 * V7X_LANES):
        cs = slice(c0, c0 + 4 * V7X_LANES)
        es = slice(d_a + c0, d_a + c0 + 4 * V7X_LANES)
        conv = cw_ref[SHORT_CONV - 1:SHORT_CONV, cs] * ext_ref[halo:halo + tbm, es]
        for dlt in range(1, SHORT_CONV):
            k = SHORT_CONV - 1 - dlt
            conv = conv + cw_ref[k:k + 1, cs] * ext_ref[halo - dlt:halo - dlt + tbm, es]
        bg = p_ref[:, d_a + c0:d_a + c0 + 4 * V7X_LANES]
        y_ref[:, d_a + c0:d_a + c0 + 4 * V7X_LANES] = (bg * conv).astype(BF16)


def _mix0(p, pool_w, pool_scale, conv_w, *, seq):
    t, four_da = p.shape
    d_a = four_da // 4
    tbm = _tile(seq, 256)
    halo = MIX0_HALO
    gc = d_a // len(POOL_WINDOWS)
    kern = functools.partial(_mix0_kernel, tbm=tbm, seq=seq, d_a=d_a)
    return pl.pallas_call(
        kern,
        out_shape=jax.ShapeDtypeStruct((t, 2 * d_a), BF16),
        grid=(t // tbm,),
        in_specs=[pl.BlockSpec((tbm, four_da), lambda i: (i, 0)),
                  pl.BlockSpec((halo, four_da), lambda i: (jnp.maximum(i * (tbm // halo) - 1, 0), 0)),
                  pl.BlockSpec((len(POOL_WINDOWS), gc, gc), lambda i: (0, 0, 0)),
                  pl.BlockSpec((1, d_a), lambda i: (0, 0)),
                  pl.BlockSpec((SHORT_CONV, d_a), lambda i: (0, 0))],
        out_specs=pl.BlockSpec((tbm, 2 * d_a), lambda i: (i, 0)),
        scratch_shapes=[pltpu.VMEM((halo + tbm, 2 * d_a), F32)],
        compiler_params=_params(("parallel",), _nbytes((tbm, four_da), F32), _nbytes((halo, four_da), F32),
                                _nbytes((tbm, 2 * d_a), BF16), _nbytes((halo + tbm, 2 * d_a), F32, 1)),
        name="mix_even",
    )(p, p, pool_w, pool_scale.reshape(1, d_a), conv_w)


def _mix1_kernel(p_ref, halo_ref, lng_ref, lnb_ref, ws_ref, sb_ref, cw_ref, cb_ref, gng_ref, gnb_ref,
                 y_ref, vn_ref, zext_ref, shift_ref, *, tbm, seq, d_c):
    row0 = pl.program_id(0) * tbm
    at_seq_start = (row0 % seq) == 0
    halo = MIX1_HALO
    hd = d_c // SGU_HEADS

    v = _gelu(p_ref[:, d_c:2 * d_c])
    mu = jnp.mean(v, axis=-1, keepdims=True)
    vc = v - mu
    var = jnp.mean(vc * vc, axis=-1, keepdims=True)
    vn_ref[...] = ((vc * lax.rsqrt(var + EPS)) * lng_ref[...] + lnb_ref[...]).astype(BF16)

    r = lax.broadcasted_iota(jnp.int32, (SGU_CHUNK, SGU_CHUNK), 0)
    c = lax.broadcasted_iota(jnp.int32, (SGU_CHUNK, SGU_CHUNK), 1)
    for h in range(SGU_HEADS):
        cs = slice(h * hd, (h + 1) * hd)
        wm = jnp.where(r >= c, ws_ref[h], 0.0).astype(BF16)
        for ch in range(tbm // SGU_CHUNK):
            rs = slice(ch * SGU_CHUNK, (ch + 1) * SGU_CHUNK)
            mixed = jnp.dot(wm, vn_ref[rs, cs], preferred_element_type=F32) + sb_ref[:, cs]
            u = _gelu(p_ref[rs, cs])
            y_ref[rs, cs] = (u * mixed).astype(BF16)

    zext_ref[0:halo, :] = jnp.where(
        at_seq_start, 0.0, halo_ref[:, 2 * d_c:3 * d_c] * jax.nn.sigmoid(halo_ref[:, 3 * d_c:4 * d_c]))
    zext_ref[halo:halo + tbm, :] = p_ref[:, 2 * d_c:3 * d_c] * jax.nn.sigmoid(p_ref[:, 3 * d_c:4 * d_c])
    gd = d_c // CONF_GROUPS
    first = halo - (CONF_KERNEL - 1)
    for gi in range(CONF_GROUPS):
        cs = slice(gi * gd, (gi + 1) * gd)
        acc = jnp.broadcast_to(cb_ref[:, cs], (tbm, gd))
        for r in range(V7X_SUBLANES):
            taps = range(r, CONF_KERNEL, V7X_SUBLANES)
            rows = taps[-1] - r + tbm
            shift_ref[0:rows, :] = zext_ref[first + r:first + r + rows, cs]
            for k in taps:
                acc = acc + cw_ref[k:k + 1, cs] * shift_ref[k - r:k - r + tbm, :]
        mu = jnp.mean(acc, axis=-1, keepdims=True)
        zc = acc - mu
        var = jnp.mean(zc * zc, axis=-1, keepdims=True)
        zn = (zc * lax.rsqrt(var + EPS)) * gng_ref[:, cs] + gnb_ref[:, cs]
        y_ref[:, d_c + gi * gd:d_c + (gi + 1) * gd] = (zn * jax.nn.sigmoid(zn)).astype(BF16)


def _mix1(p, ln_g, ln_b, sgu_w, sgu_b, conv_w, conv_b, gn_g, gn_b, *, seq):
    t, four_dc = p.shape
    d_c = four_dc // 4
    tbm = _tile(seq, 256)
    halo = MIX1_HALO
    hd = d_c // SGU_HEADS
    sb = jnp.repeat(sgu_b.T, hd, axis=1)
    row = lambda a: a.reshape(1, d_c)
    kern = functools.partial(_mix1_kernel, tbm=tbm, seq=seq, d_c=d_c)
    const2 = lambda i: (0, 0)
    return pl.pallas_call(
        kern,
        out_shape=jax.ShapeDtypeStruct((t, 2 * d_c), BF16),
        grid=(t // tbm,),
        in_specs=[pl.BlockSpec((tbm, four_dc), lambda i: (i, 0)),
                  pl.BlockSpec((halo, four_dc), lambda i: (jnp.maximum(i * (tbm // halo) - 1, 0), 0)),
                  pl.BlockSpec((1, d_c), const2),
                  pl.BlockSpec((1, d_c), const2),
                  pl.BlockSpec((SGU_HEADS, SGU_CHUNK, SGU_CHUNK), lambda i: (0, 0, 0)),
                  pl.BlockSpec((SGU_CHUNK, d_c), const2),
                  pl.BlockSpec((CONF_KERNEL, d_c), const2),
                  pl.BlockSpec((1, d_c), const2),
                  pl.BlockSpec((1, d_c), const2),
                  pl.BlockSpec((1, d_c), const2)],
        out_specs=pl.BlockSpec((tbm, 2 * d_c), lambda i: (i, 0)),
        scratch_shapes=[pltpu.VMEM((tbm, d_c), BF16), pltpu.VMEM((halo + tbm, d_c), F32),
                        pltpu.VMEM((halo + tbm, d_c // CONF_GROUPS), F32)],
        compiler_params=_params(("parallel",), _nbytes((tbm, four_dc), F32), _nbytes((halo, four_dc), F32),
                                _nbytes((tbm, 2 * d_c), BF16), _nbytes((halo + tbm, d_c), F32, 1),
                                _nbytes((SGU_CHUNK, d_c), F32)),
        name="mix_odd",
    )(p, p, row(ln_g), row(ln_b), sgu_w, sb, conv_w, row(conv_b), row(gn_g), row(gn_b))


_NO_ID = float(2 ** 20)


def _sorted_top16(x, sub):
    ng = PEER_NKEYS // V7X_SUBLANES
    xs = [x[v * V7X_SUBLANES:(v + 1) * V7X_SUBLANES, :] for v in range(ng)]
    ids = [sub + float(v * V7X_SUBLANES) for v in range(ng)]
    for rnd in range(ng):
        for v in range(rnd % 2, ng - 1, 2):
            a, b, ia, ib = xs[v], xs[v + 1], ids[v], ids[v + 1]
            swap = b > a
            xs[v], xs[v + 1] = jnp.maximum(a, b), jnp.minimum(a, b)
            ids[v], ids[v + 1] = jnp.where(swap, ib, ia), jnp.where(swap, ia, ib)
    vals, picks = [], []
    for n in range(PEER_TOPK):
        m = jnp.max(xs[0], axis=0, keepdims=True)
        sel = jnp.min(jnp.where(xs[0] == m, ids[0], _NO_ID), axis=0, keepdims=True)
        vals.append(m)
        picks.append(sel)
        win = ids[0] == sel
        for v in range(PEER_TOPK - 1 - n):
            xs[v] = jnp.where(win, xs[v + 1], xs[v])
            ids[v] = jnp.where(win, ids[v + 1], ids[v])
    return vals, picks


def _pair_top16(s0, i0, s1, i1, sub):
    half = PEER_TOPK // 2
    s0_lo, s0_hi = jnp.concatenate(s0[:half], axis=0), jnp.concatenate(s0[half:], axis=0)
    e0_lo = jnp.concatenate(i0[:half], axis=0) * float(PEER_NKEYS)
    e0_hi = jnp.concatenate(i0[half:], axis=0) * float(PEER_NKEYS)
    slot_v, slot_e = [], []
    for l in range(PEER_TOPK):
        v = s0_lo + s1[l]
        rows_valid = PEER_TOPK // (l + 1)
        if rows_valid < half:
            v = jnp.where(sub < float(rows_valid), v, -jnp.inf)
        slot_v.append(v)
        slot_e.append(e0_lo + i1[l])
    hi_v, hi_e = s0_hi + s1[0], e0_hi + i1[0]
    pos_lo = sub * float(PEER_TOPK)
    pos_hi = (sub + float(half)) * float(PEER_TOPK)
    sums, experts = [], []
    for n in range(PEER_TOPK):
        m = jnp.max(jnp.maximum(slot_v[0], hi_v), axis=0, keepdims=True)
        pick = jnp.min(jnp.minimum(jnp.where(slot_v[0] == m, pos_lo, _NO_ID),
                                   jnp.where(hi_v == m, pos_hi, _NO_ID)), axis=0, keepdims=True)
        win_lo, win_hi = pos_lo == pick, pos_hi == pick
        experts.append(jnp.max(jnp.maximum(jnp.where(win_lo, slot_e[0], -1.0),
                                           jnp.where(win_hi, hi_e, -1.0)), axis=0, keepdims=True))
        sums.append(m)
        pos_lo = jnp.where(win_lo, pos_lo + 1.0, pos_lo)
        hi_v = jnp.where(win_hi, -jnp.inf, hi_v)
        for l in range(PEER_TOPK - 1 - n):
            slot_v[l] = jnp.where(win_lo, slot_v[l + 1], slot_v[l])
            slot_e[l] = jnp.where(win_lo, slot_e[l + 1], slot_e[l])
    return sums, experts


def _select_kernel(h_ref, y_ref, wo_ref, g_ref, wq_ref, keys_ref, h1_ref, xn_ref, e_ref, gate_ref,
                   qt_ref, fs_ref, fe_ref, *, tbs):
    x = h_ref[...] + jnp.dot(y_ref[...], wo_ref[...], preferred_element_type=F32)
    h1_ref[...] = x
    ms = jnp.mean(x * x, axis=-1, keepdims=True)
    xn = (x * lax.rsqrt(ms + EPS) * g_ref[...]).astype(BF16)
    xn_ref[...] = xn
    qt_ref[...] = lax.dot_general(wq_ref[...], xn, (((1,), (1,)), ((), ())),
                                  preferred_element_type=F32).astype(BF16)

    sub =lax.broadcasted_iota(jnp.int32, (V7X_SUBLANES, tbs), 0).astype(F32)

    def head(hd, carry):
        tops = []
        for part in range(2):
            hp = hd * 2 + part
            q = qt_ref[pl.ds(pl.multiple_of(hp * PEER_DKH, PEER_DKH), PEER_DKH), :]
            scores = jnp.dot(keys_ref[hp], q, preferred_element_type=F32)
            tops.append(_sorted_top16(scores, sub))
        (s0, i0), (s1, i1) = tops
        f_s, f_e = _pair_top16(s0, i0, s1, i1, sub)
        f_s = jnp.concatenate(f_s, axis=0)
        ex = jnp.exp(f_s - f_s[0:1])
        gates = ex / jnp.sum(ex, axis=0, keepdims=True)
        rows = pl.ds(pl.multiple_of(hd * PEER_TOPK, PEER_TOPK), PEER_TOPK)
        fs_ref[rows, :] = gates
        fe_ref[rows, :] = jnp.concatenate(f_e, axis=0)
        return carry

    lax.fori_loop(0, PEER_HEADS, head, 0, unroll=PEER_HEADS)
    gate_ref[...] = fs_ref[...].T
    e_ref[...] = fe_ref[...].T


def _select(h, y, w_out, g, wq_t, keys):
    t, d = h.shape
    ky = y.shape[1]
    tbs = _tile(t, 256)
    kern = functools.partial(_select_kernel, tbs=tbs)
    nq = wq_t.shape[0]
    return pl.pallas_call(
        kern,
        out_shape=(jax.ShapeDtypeStruct((t, d), F32),
                   jax.ShapeDtypeStruct((t, d), BF16),
                   jax.ShapeDtypeStruct((t, N_SEL), F32),
                   jax.ShapeDtypeStruct((t, N_SEL), F32)),
        grid=(t // tbs,),
        in_specs=[pl.BlockSpec((tbs, d), lambda i: (i, 0)),
                  pl.BlockSpec((tbs, ky), lambda i: (i, 0)),
                  pl.BlockSpec((ky, d), lambda i: (0, 0)),
                  pl.BlockSpec((1, d), lambda i: (0, 0)),
                  pl.BlockSpec((nq, d), lambda i: (0, 0)),
                  pl.BlockSpec(keys.shape, lambda i: (0, 0, 0))],
        out_specs=(pl.BlockSpec((tbs, d), lambda i: (i, 0)),
                   pl.BlockSpec((tbs, d), lambda i: (i, 0)),
                   pl.BlockSpec((tbs, N_SEL), lambda i: (i, 0)),
                   pl.BlockSpec((tbs, N_SEL), lambda i: (i, 0))),
        scratch_shapes=[pltpu.VMEM((nq, tbs), BF16),
                        pltpu.VMEM((N_SEL, tbs), F32),
                        pltpu.VMEM((N_SEL, tbs), F32)],
        compiler_params=_params(("parallel",), 2 * _nbytes((tbs, d), F32), _nbytes((nq, d), BF16),
                                _nbytes((tbs, d), BF16), _nbytes(keys.shape, BF16),
                                _nbytes((tbs, ky), BF16), _nbytes((ky, d), BF16)),
        name="peer_select",
    )(h, y, w_out, g.reshape(1, d), wq_t, keys)


def _gates_kernel(e_ref, gate_ref, o_ref, tile_ref, *, tbg, pitch):
    sub_id = lax.broadcasted_iota(jnp.int32, (PEER_NKEYS, N_SEL), 0)

    def token(tk, carry):
        e = e_ref[pl.ds(tk, 1), :].astype(jnp.int32)
        gate = gate_ref[pl.ds(tk, 1), :]
        hot_i = jnp.where(sub_id == (e >> KEY_BITS), 1.0, 0.0).astype(BF16)
        hot_j = jnp.where(sub_id == (e & (PEER_NKEYS - 1)), gate, 0.0).astype(BF16)
        tile = lax.dot_general(hot_i, hot_j, (((1,), (1,)), ((), ())), preferred_element_type=F32)
        tile_ref[pl.ds(tk, PEER_NKEYS, stride=pitch), :] = tile
        return carry

    lax.fori_loop(0, tbg, token, 0, unroll=TOKEN_UNROLL)
    for i in range(PEER_NKEYS):
        o_ref[:, i * PEER_NKEYS:(i + 1) * PEER_NKEYS] = tile_ref[i * pitch:i * pitch + tbg, :].astype(BF16)


def _dense_gates(e, gate):
    t = e.shape[0]
    n_exp = PEER_NKEYS * PEER_NKEYS
    tbg = _tile(t, 128)
    pitch = tbg + V7X_SUBLANES
    kern = functools.partial(_gates_kernel, tbg=tbg, pitch=pitch)
    return pl.pallas_call(
        kern,
        out_shape=jax.ShapeDtypeStruct((t, n_exp), BF16),
        grid=(t // tbg,),
        in_specs=[pl.BlockSpec((tbg, N_SEL), lambda i: (i, 0)),
                  pl.BlockSpec((tbg, N_SEL), lambda i: (i, 0))],
        out_specs=pl.BlockSpec((tbg, n_exp), lambda i: (i, 0)),
        scratch_shapes=[pltpu.VMEM((PEER_NKEYS * pitch, PEER_NKEYS), F32)],
        compiler_params=_params(("parallel",), _nbytes((tbg, n_exp), BF16),
                                _nbytes((PEER_NKEYS * pitch, PEER_NKEYS), F32, 1)),
        name="peer_dense_gates",
    )(e, gate)


def _ffn_kernel(xn_ref, u_ref, v_ref, gt_ref, h_ref, fg_ref, o_ref, *, final_norm):
    n = pl.program_id(1)

    @pl.when(n == 0)
    def _():
        o_ref[...] = h_ref[...]

    s = lax.dot_general(xn_ref[...], u_ref[...], (((1,), (1,)), ((), ())), preferred_element_type=F32)
    act = (_gelu(s) * gt_ref[...].astype(F32)).astype(BF16)
    o_ref[...] += jnp.dot(act, v_ref[...], preferred_element_type=F32)

    if final_norm:
        @pl.when(n == pl.num_programs(1) - 1)
        def _():
            x = o_ref[...]
            ms = jnp.mean(x * x, axis=-1, keepdims=True)
            o_ref[...] = x * lax.rsqrt(ms + EPS) * fg_ref[...]


def _peer_ffn(xn, u, v, gt, h, final_g, *, final_norm):
    t, d = xn.shape
    n_exp = u.shape[0]
    tb, nb = _tile(t, 1024), _tile(n_exp, 1024)
    kern = functools.partial(_ffn_kernel, final_norm=final_norm)
    return pl.pallas_call(
        kern,
        out_shape=jax.ShapeDtypeStruct((t, d), F32),
        grid=(t // tb, n_exp // nb),
        in_specs=[pl.BlockSpec((tb, d), lambda i, n: (i, 0), pipeline_mode=pl.Buffered(1)),
                  pl.BlockSpec((nb, d), lambda i, n: (n, 0)),
                  pl.BlockSpec((nb, d), lambda i, n: (n, 0)),
                  pl.BlockSpec((tb, nb), lambda i, n: (i, n)),
                  pl.BlockSpec((tb, d), lambda i, n: (i, 0), pipeline_mode=pl.Buffered(1)),
                  pl.BlockSpec((1, d), lambda i, n: (0, 0))],
        out_specs=pl.BlockSpec((tb, d), lambda i, n: (i, 0)),
        compiler_params=_params(("parallel", "arbitrary"), _nbytes((tb, d), BF16, 1), 2 * _nbytes((nb, d), BF16),
                                _nbytes((tb, nb), BF16), _nbytes((tb, d), F32, 1), _nbytes((tb, d), F32),
                                _nbytes((tb, nb), F32, 1)),
        name="peer_ffn",
    )(xn, u, v, gt, h, final_g.reshape(1, d))


def kernel(x, mix_norm_g, ffn_norm_g, ev_w_in, ev_pool_w, ev_pool_scale, ev_conv_w, ev_w_out, od_w_in,
           od_v_norm_g, od_v_norm_b, od_sgu_w, od_sgu_b, od_conv_w, od_conv_b, od_gn_g, od_gn_b, od_w_out,
           peer_w_q, peer_subkeys, peer_u, peer_v, final_norm_g):
    bsz, seq, d = x.shape
    depth = mix_norm_g.shape[0]
    h = x.reshape(bsz * seq, d)
    for layer in range(depth):
        j = layer // 2
        if layer % 2 == 0:
            p = _norm_matmul(h, mix_norm_g[layer], ev_w_in[j].astype(BF16))
            y = _mix0(p, ev_pool_w[j].astype(BF16), ev_pool_scale[j], ev_conv_w[j], seq=seq)
            w_out = ev_w_out[j].astype(BF16)
        else:
            p = _norm_matmul(h, mix_norm_g[layer], od_w_in[j].astype(BF16))
            y = _mix1(p, od_v_norm_g[j], od_v_norm_b[j], od_sgu_w[j], od_sgu_b[j], od_conv_w[j],
                      od_conv_b[j], od_gn_g[j], od_gn_b[j], seq=seq)
            w_out = od_w_out[j].astype(BF16)
        keys = peer_subkeys[layer].reshape(2 * PEER_HEADS, PEER_NKEYS, PEER_DKH).astype(BF16)
        h, xn, e, gate = _select(h, y, w_out, ffn_norm_g[layer], peer_w_q[layer].T.astype(BF16), keys)
        gt = _dense_gates(e, gate)
        h = _peer_ffn(xn, _layer_to_bf16(peer_u, layer), _layer_to_bf16(peer_v, layer), gt, h, final_norm_g,
                      final_norm=(layer == depth - 1))
    return h.reshape(bsz, seq, d)
```
